```python
import math
import jax, jax.numpy as jnp
from jax import lax
import numpy as np

D_MODEL = 2048
BATCH = 4
SEQ = 2048
DEPTH = 4

GRID_W = 64
CTX_LEN = 256
N_MIXERS = 3
N_HEADS = 16
HEAD_DIM = 64
V_HEAD_DIM = 2 * HEAD_DIM
ROPE_BASE = 10000.0
AXIS_ROT = HEAD_DIM // 2
Q_BLOCK = 128
SUBLN_EPS = 1e-5
N_FFT_GROUPS = 4
POOL_WINDOWS = (2, 4, 8, 16)
N_POOL_GROUPS = len(POOL_WINDOWS)
D_FF = 4 * D_MODEL
NORM_EPS = 1e-6
N_MOD = 6
N_A = len(range(0, DEPTH, N_MIXERS))
N_B = len(range(1, DEPTH, N_MIXERS))
N_C = len(range(2, DEPTH, N_MIXERS))

kernel_name = 'hybrid_diffattn_fourier_pool_dit'

F32 = jnp.float32


def rmsnorm(x, g, eps=NORM_EPS):
    x32 = x.astype(F32)
    y = x32 * lax.rsqrt(jnp.mean(x32 * x32, axis=-1, keepdims=True) + eps)
    return y.astype(x.dtype) * g


def ada_mod(cond, w, b):
    m = jax.nn.silu(cond) @ w + b
    m = m.reshape(m.shape[:-1] + (N_MOD, 1, D_MODEL))
    return [m[..., k, :, :] for k in range(N_MOD)]


def axial_rope_tables(seq_len):
    rows = seq_len // GRID_W
    row = jnp.repeat(jnp.arange(rows), GRID_W).astype(F32)
    col = jnp.tile(jnp.arange(GRID_W), rows).astype(F32)
    n_freq = AXIS_ROT // 2
    inv = 1.0 / (ROPE_BASE ** (jnp.arange(n_freq, dtype=F32) / n_freq))
    ang = jnp.stack([row[:, None] * inv, col[:, None] * inv], axis=1)
    return jnp.cos(ang), jnp.sin(ang)


def apply_axial_rope(x, cos, sin):
    xs = x.reshape(x.shape[:-1] + (2, 2, AXIS_ROT // 2))
    a = xs[..., 0, :]
    b = xs[..., 1, :]
    cs = cos[None, :, None, None].astype(x.dtype)
    sn = sin[None, :, None, None].astype(x.dtype)
    out = jnp.stack([a * cs - b * sn, a * sn + b * cs], axis=-2)
    return out.reshape(x.shape)


def diff_attention(ul, uc, cos, sin, w_qkv, w_o, lq1, lk1, lq2, lk2, g_subln, lambda_init, with_ctx_out):
    B, L, _ = ul.shape

    def proj(u):
        q, k, v = jnp.split(u @ w_qkv, 3, axis=-1)
        sh = u.shape[:2]
        return (q.reshape(sh + (N_HEADS, 2, HEAD_DIM)),
                k.reshape(sh + (N_HEADS, 2, HEAD_DIM)),
                v.reshape(sh + (N_HEADS, V_HEAD_DIM)))

    ql, kl, vl = proj(ul)
    qc, kc, vc = proj(uc)
    ql = apply_axial_rope(ql, cos, sin)
    kl = apply_axial_rope(kl, cos, sin)
    lam = (jnp.exp(jnp.sum(lq1.astype(F32) * lk1.astype(F32)))
           - jnp.exp(jnp.sum(lq2.astype(F32) * lk2.astype(F32))) + lambda_init)
    scale = 1.0 / math.sqrt(HEAD_DIM)

    def attend(q, k, v):
        s = jnp.einsum('bqhcd,bkhcd->bhcqk', q, k).astype(F32) * scale
        p = jax.nn.softmax(s, axis=-1)
        a = (p[:, :, 0] - lam * p[:, :, 1]).astype(v.dtype)
        return jnp.einsum('bhqk,bkhe->bqhe', a, v)

    k_all = jnp.concatenate([kl, kc], axis=1)
    v_all = jnp.concatenate([vl, vc], axis=1)
    nb = L // Q_BLOCK
    qb = ql.reshape(B, nb, Q_BLOCK, N_HEADS, 2, HEAD_DIM).swapaxes(0, 1)
    ol = lax.map(lambda q: attend(q, k_all, v_all), qb)
    ol = ol.swapaxes(0, 1).reshape(B, L, N_HEADS, V_HEAD_DIM)

    def finish(o):
        o = rmsnorm(o, g_subln, SUBLN_EPS) * (1.0 - lambda_init)
        return o.reshape(o.shape[:2] + (N_HEADS * V_HEAD_DIM,)) @ w_o

    yl = finish(ol)
    yc = finish(attend(qc, kc, vc)) if with_ctx_out else None
    return yl, yc


def fourier_mix(u, w_out):
    B, L, D = u.shape
    ug = u.reshape(B, L, N_FFT_GROUPS, D // N_FFT_GROUPS).astype(F32)
    f = jnp.fft.fft2(ug, axes=(1, 3), norm='ortho').real
    return f.astype(u.dtype).reshape(B, L, D) @ w_out


def pool_mix(u, w_pool, pool_scale):
    L = u.shape[1]
    cg = D_MODEL // N_POOL_GROUPS
    t = jnp.arange(L)
    outs = []
    for g, w in enumerate(POOL_WINDOWS):
        ug = u[..., g * cg:(g + 1) * cg]
        cs = jnp.cumsum(ug.astype(F32), axis=1)
        cs = jnp.concatenate([jnp.zeros_like(cs[:, :1]), cs], axis=1)
        lo = jnp.clip(t - w // 2, 0, L)
        hi = jnp.clip(t - w // 2 + w, 0, L)
        cnt = (hi - lo).astype(F32)[None, :, None]
        mean = (jnp.take(cs, hi, axis=1) - jnp.take(cs, lo, axis=1)) / cnt
        outs.append((mean.astype(u.dtype) - ug) @ w_pool[g])
    return jnp.concatenate(outs, axis=-1) * pool_scale


def sq_relu_mlp(u, w1, w2):
    h = jax.nn.relu(u @ w1)
    return (h * h) @ w2


def setup_inputs(seed: int = 0) -> dict:
    key = jax.random.key(seed)
    ks = jax.random.split(key, 24)
    D = D_MODEL
    cg = D // N_POOL_GROUPS
    nrm = lambda k, shape, s: jax.random.normal(k, shape, F32) * s
    return {
        'x': nrm(ks[0], (BATCH, SEQ, D), 1.0),
        'c': nrm(ks[1], (BATCH, D), 1.0),
        'ctx': nrm(ks[2], (BATCH, CTX_LEN, D), 1.0),
        'c_ctx': nrm(ks[3], (D,), 1.0),
        'w_mod': nrm(ks[4], (DEPTH, D, N_MOD * D), 0.5 * D ** -0.5),
        'b_mod': nrm(ks[5], (DEPTH, N_MOD * D), 0.01),
        'g_mix_pre': 1.0 + nrm(ks[6], (DEPTH, D), 0.02),
        'g_mix_post': 1.0 + nrm(ks[7], (DEPTH, D), 0.02),
        'g_mlp_pre': 1.0 + nrm(ks[8], (DEPTH, D), 0.02),
        'g_mlp_post': 1.0 + nrm(ks[9], (DEPTH, D), 0.02),
        'w_qkv': nrm(ks[10], (N_A, D, 3 * D), D ** -0.5),
        'w_attn_out': nrm(ks[11], (N_A, N_HEADS * V_HEAD_DIM, D), (N_HEADS * V_HEAD_DIM) ** -0.5),
        'lambda_q1': nrm(ks[12], (N_A, HEAD_DIM), 0.1),
        'lambda_k1': nrm(ks[13], (N_A, HEAD_DIM), 0.1),
        'lambda_q2': nrm(ks[14], (N_A, HEAD_DIM), 0.1),
        'lambda_k2': nrm(ks[15], (N_A, HEAD_DIM), 0.1),
        'g_subln': 1.0 + nrm(ks[16], (N_A, V_HEAD_DIM), 0.02),
        'w_fourier_out': nrm(ks[17], (N_B, D, D), D ** -0.5),
        'w_pool': nrm(ks[18], (N_C, N_POOL_GROUPS, cg, cg), cg ** -0.5),
        'pool_scale': 1.0 + nrm(ks[19], (N_C, D), 0.1),
        'w_mlp_in': nrm(ks[20], (DEPTH, D, D_FF), D ** -0.5),
        'w_mlp_out': nrm(ks[21], (DEPTH, D_FF, D), D_FF ** -0.5),
    }


def reference(x, c, ctx, c_ctx, w_mod, b_mod, g_mix_pre, g_mix_post, g_mlp_pre, g_mlp_post,
              w_qkv, w_attn_out, lambda_q1, lambda_k1, lambda_q2, lambda_k2, g_subln,
              w_fourier_out, w_pool, pool_scale, w_mlp_in, w_mlp_out):
    xl, xc = x, ctx
    cos, sin = axial_rope_tables(xl.shape[1])
    ia = ib = ic = 0
    for i in range(DEPTH):
        last = i == DEPTH - 1
        kind = i % N_MIXERS
        need_ctx_in = (not last) or kind == 0
        sh1, sc1, gt1, sh2, sc2, gt2 = ada_mod(c, w_mod[i], b_mod[i])
        csh1, csc1, cgt1, csh2, csc2, cgt2 = ada_mod(c_ctx, w_mod[i], b_mod[i])

        ul = rmsnorm(xl, g_mix_pre[i]) * (1.0 + sc1) + sh1
        uc = rmsnorm(xc, g_mix_pre[i]) * (1.0 + csc1) + csh1 if need_ctx_in else None
        if kind == 0:
            lambda_init = 0.8 - 0.6 * math.exp(-0.3 * i)
            yl, yc = diff_attention(ul, uc, cos, sin, w_qkv[ia], w_attn_out[ia],
                                    lambda_q1[ia], lambda_k1[ia], lambda_q2[ia], lambda_k2[ia],
                                    g_subln[ia], lambda_init, not last)
            ia += 1
        elif kind == 1:
            yl = fourier_mix(ul, w_fourier_out[ib])
            yc = None if last else fourier_mix(uc, w_fourier_out[ib])
            ib += 1
        else:
            yl = pool_mix(ul, w_pool[ic], pool_scale[ic])
            yc = None if last else pool_mix(uc, w_pool[ic], pool_scale[ic])
            ic += 1
        xl = xl + gt1 * rmsnorm(yl, g_mix_post[i])

        vl = rmsnorm(xl, g_mlp_pre[i]) * (1.0 + sc2) + sh2
        xl = xl + gt2 * rmsnorm(sq_relu_mlp(vl, w_mlp_in[i], w_mlp_out[i]), g_mlp_post[i])

        if not last:
            xc = xc + cgt1 * rmsnorm(yc, g_mix_post[i])
            vc = rmsnorm(xc, g_mlp_pre[i]) * (1.0 + csc2) + csh2
            xc = xc + cgt2 * rmsnorm(sq_relu_mlp(vc, w_mlp_in[i], w_mlp_out[i]), g_mlp_post[i])
    return xl
```

```python
import functools
import math

import jax
import jax.numpy as jnp
from jax import lax
from jax.experimental import pallas as pl
from jax.experimental.pallas import tpu as pltpu

F32 = jnp.float32
BF16 = jnp.bfloat16

GRID_W = 64
N_MIXERS = 3
N_HEADS = 16
HEAD_DIM = 64
V_HEAD_DIM = 2 * HEAD_DIM
ROPE_BASE = 10000.0
AXIS_ROT = HEAD_DIM // 2
SUBLN_EPS = 1e-5
N_FFT_GROUPS = 4
POOL_WINDOWS = (2, 4, 8, 16)
NORM_EPS = 1e-6
N_MOD = 6
POOL_HALO = 8

LANES = 128
V7X_VMEM_BYTES = 64 * 1024 * 1024
VMEM_LIMIT = 56 * 1024 * 1024


def _cparams(n_axes):
    return pltpu.CompilerParams(
        dimension_semantics=("arbitrary",) * n_axes, vmem_limit_bytes=VMEM_LIMIT)


def _tile(pref, *counts):
    t = min(pref, *counts)
    while t > 8 and any(c % t for c in counts):
        t -= 8
    assert t >= 8 and all(c % t == 0 for c in counts), (pref, counts)
    return t


def _rms(x, g, eps):
    return x * lax.rsqrt(jnp.mean(x * x, axis=-1, keepdims=True) + eps) * g


def _mod_row(mod_ref, r, k, d):
    return mod_ref[pl.ds(r, 1), k * d:(k + 1) * d]


def _prenorm(x, g, mod_ref, r, k_shift, d):
    sh = _mod_row(mod_ref, r, k_shift, d)
    sc = _mod_row(mod_ref, r, k_shift + 1, d)
    return _rms(x, g, NORM_EPS) * (1.0 + sc) + sh


def _mod_kernel(cond_ref, w_ref, b_ref, o_ref):
    s = jax.nn.silu(cond_ref[...]).astype(BF16)
    o_ref[...] = jnp.dot(s, w_ref[...].astype(BF16), preferred_element_type=F32) + b_ref[...]


def _mod_all(cond, w_mod, b_mod):
    depth, d, n = w_mod.shape
    rows = cond.shape[0]
    tn = _tile(1024, n)
    return pl.pallas_call(
        _mod_kernel,
        out_shape=jax.ShapeDtypeStruct((depth, rows, n), F32),
        grid=(depth, n // tn),
        in_specs=[
            pl.BlockSpec((rows, d), lambda l, j: (0, 0)),
            pl.BlockSpec((None, d, tn), lambda l, j: (l, 0, j)),
            pl.BlockSpec((None, 1, tn), lambda l, j: (l, 0, j)),
        ],
        out_specs=pl.BlockSpec((None, rows, tn), lambda l, j: (l, 0, j)),
        compiler_params=_cparams(2),
        name="ada_mod",
    )(cond, w_mod, b_mod.reshape(depth, 1, n))


def _qkv_kernel(x_ref, mod_ref, g_ref, w_ref, cos_ref, sa_ref, sb_ref, o_ref, u_ref,
                *, tm, tn, seq, n_batch, n_lat_tiles, n_q_j, n_rope_j):
    i = pl.program_id(0)
    j = pl.program_id(1)
    d = x_ref.shape[1]

    @pl.when(j == 0)
    def _():
        r = jnp.minimum(lax.div(i * tm, seq), n_batch)
        u_ref[...] = _prenorm(x_ref[...], g_ref[...], mod_ref, r, 0, d).astype(BF16)

    y = jnp.dot(u_ref[...], w_ref[...], preferred_element_type=F32)
    scale = jnp.where(j < n_q_j, 1.0 / math.sqrt(HEAD_DIM), 1.0).astype(F32)
    is_rope = jnp.logical_and(j < n_rope_j, i < n_lat_tiles)

    @pl.when(is_rope)
    def _():
        cos = cos_ref[...]
        sa = sa_ref[...]
        sb = sb_ref[...]
        for c in range(tn // LANES):
            yc = y[:, c * LANES:(c + 1) * LANES]
            rot = (yc * cos + pltpu.roll(yc, LANES - AXIS_ROT // 2, 1) * sa
                   + pltpu.roll(yc, AXIS_ROT // 2, 1) * sb)
            o_ref[c] = (rot * scale).astype(BF16)

    @pl.when(jnp.logical_not(is_rope))
    def _():
        for c in range(tn // LANES):
            o_ref[c] = (y[:, c * LANES:(c + 1) * LANES] * scale).astype(BF16)


def _qkv_proj(xu, mod, g, w, rope, *, layer, seq, n_batch, n_lat_rows):
    rows, d = xu.shape
    n = w.shape[1]
    tm = _tile(1024, seq, rows - n_lat_rows)
    tn = _tile(512, n // 3)
    assert tn % LANES == 0
    lat_tiles_per_seq = seq // tm
    kern = functools.partial(
        _qkv_kernel, tm=tm, tn=tn, seq=seq, n_batch=n_batch, n_lat_tiles=n_lat_rows // tm,
        n_q_j=n // 3 // tn, n_rope_j=2 * n // 3 // tn)
    tab_spec = pl.BlockSpec((tm, LANES), lambda i, j: (i % lat_tiles_per_seq, 0))
    return pl.pallas_call(
        kern,
        out_shape=jax.ShapeDtypeStruct((n // LANES, rows, LANES), BF16),
        grid=(rows // tm, n // tn),
        in_specs=[
            pl.BlockSpec((tm, d), lambda i, j: (i, 0)),
            pl.BlockSpec((None,) + mod.shape[1:], lambda i, j: (layer, 0, 0)),
            pl.BlockSpec((None, 1, d), lambda i, j: (layer, 0, 0)),
            pl.BlockSpec((d, tn), lambda i, j: (0, j)),
            tab_spec, tab_spec, tab_spec,
        ],
        out_specs=pl.BlockSpec((tn // LANES, tm, LANES), lambda i, j: (j, i, 0)),
        scratch_shapes=[pltpu.VMEM((tm, d), BF16)],
        compiler_params=_cparams(2),
        name="qkv_proj",
    )(xu, mod, g, w, *rope)


def _rope_tables(seq):
    rows = seq // GRID_W
    row = jnp.repeat(jnp.arange(rows), GRID_W).astype(F32)
    col = jnp.tile(jnp.arange(GRID_W), rows).astype(F32)
    n_freq = AXIS_ROT // 2
    inv = 1.0 / (ROPE_BASE ** (jnp.arange(n_freq, dtype=F32) / n_freq))
    ang_r = row[:, None] * inv
    ang_c = col[:, None] * inv
    cr, sr, cc, sc = jnp.cos(ang_r), jnp.sin(ang_r), jnp.cos(ang_c), jnp.sin(ang_c)
    z = jnp.zeros_like(sr)
    reps = LANES // HEAD_DIM
    cos = jnp.tile(jnp.concatenate([cr, cr, cc, cc], axis=1), (1, reps))
    sa = jnp.tile(jnp.concatenate([-sr, z, -sc, z], axis=1), (1, reps))
    sb = jnp.tile(jnp.concatenate([z, sr, z, sc], axis=1), (1, reps))
    return cos, sa, sb


def _attn_kernel(*refs, n_seg, lam_init):
    q_ref = refs[0]
    seg_refs = refs[1:1 + 2 * n_seg]
    lq1_ref, lk1_ref, lq2_ref, lk2_ref, gs_ref, o_ref, kk_ref, vv_ref = refs[1 + 2 * n_seg:]

    @pl.when(pl.program_id(2) == 0)
    def _():
        off = 0
        for s in range(n_seg):
            n = seg_refs[2 * s].shape[0]
            kk_ref[off:off + n, :] = seg_refs[2 * s][...]
            vv_ref[off:off + n, :] = seg_refs[2 * s + 1][...]
            off += n

    q = q_ref[...]
    k = kk_ref[...]
    v = vv_ref[...]
    lane = lax.broadcasted_iota(jnp.int32, q.shape, 1)
    comps = []
    for c in range(2):
        mask = (lane < HEAD_DIM) if c == 0 else (lane >= HEAD_DIM)
        qc = jnp.where(mask, q, jnp.zeros_like(q))
        s = lax.dot_general(qc, k, (((1,), (1,)), ((), ())), preferred_element_type=F32)
        m = jnp.max(s, axis=-1, keepdims=True)
        p = jnp.exp(s - m)
        l = jnp.sum(p, axis=-1, keepdims=True)
        acc = jnp.dot(p.astype(BF16), v, preferred_element_type=F32)
        comps.append(acc / l)
    lam = (jnp.exp(jnp.sum(lq1_ref[...] * lk1_ref[...], axis=-1, keepdims=True))
           - jnp.exp(jnp.sum(lq2_ref[...] * lk2_ref[...], axis=-1, keepdims=True)) + lam_init)
    o = comps[0] - lam * comps[1]
    o = _rms(o, gs_ref[...], SUBLN_EPS) * (1.0 - lam_init)
    o_ref[...] = o.astype(BF16)


def _attention(qkv, lam_params, gs, *, ia, lam_init, n_batch, q_len, q_row0, segs):
    n_cols = qkv.shape[0]
    n_heads = n_cols // 3
    tq = _tile(512, q_len)
    n_qt = q_len // tq
    nk = sum(n for _, n in segs)
    q_blk0 = q_row0 // tq
    in_specs = [pl.BlockSpec((None, tq, LANES),
                             lambda b, h, qi: (h, q_blk0 + b * n_qt + qi, 0))]
    args = [qkv]
    for row0, n in segs:
        blk0 = row0 // n
        assert row0 % n == 0
        in_specs.append(pl.BlockSpec(
            (None, n, LANES), lambda b, h, qi, blk0=blk0: (n_heads + h, blk0 + b, 0)))
        in_specs.append(pl.BlockSpec(
            (None, n, LANES), lambda b, h, qi, blk0=blk0: (2 * n_heads + h, blk0 + b, 0)))
        args += [qkv, qkv]
    for p in lam_params:
        in_specs.append(pl.BlockSpec((None, 1, HEAD_DIM), lambda b, h, qi: (ia, 0, 0)))
        args.append(p)
    in_specs.append(pl.BlockSpec((None, 1, V_HEAD_DIM), lambda b, h, qi: (ia, 0, 0)))
    args.append(gs)
    return pl.pallas_call(
        functools.partial(_attn_kernel, n_seg=len(segs), lam_init=lam_init),
        out_shape=jax.ShapeDtypeStruct((n_batch * q_len, n_heads * V_HEAD_DIM), BF16),
        grid=(n_batch, n_heads, n_qt),
        in_specs=in_specs,
        out_specs=pl.BlockSpec((tq, V_HEAD_DIM), lambda b, h, qi: (b * n_qt + qi, h)),
        scratch_shapes=[pltpu.VMEM((nk, LANES), BF16), pltpu.VMEM((nk, LANES), BF16)],
        compiler_params=_cparams(3),
        name="diff_attn",
    )(*args)


def _proj_res_kernel(*refs, tm, seq, n_batch, n_lat_tiles, has_ctx):
    if has_ctx:
        inl_ref, inc_ref, x_ref, mod_ref, g_ref, w_ref, o_ref = refs
    else:
        inl_ref, x_ref, mod_ref, g_ref, w_ref, o_ref = refs
    i = pl.program_id(0)
    d = x_ref.shape[1]

    def body(inp):
        y = jnp.dot(inp, w_ref[...], preferred_element_type=F32)
        r = jnp.minimum(lax.div(i * tm, seq), n_batch)
        gate = _mod_row(mod_ref, r, 2, d)
        o_ref[...] = x_ref[...] + gate * _rms(y, g_ref[...], NORM_EPS)

    if has_ctx:
        @pl.when(i < n_lat_tiles)
        def _():
            body(inl_ref[...])

        @pl.when(i >= n_lat_tiles)
        def _():
            body(inc_ref[...])
    else:
        body(inl_ref[...])


def _proj_res(inp_l, inp_c, xu, mod, g, w, *, layer, seq, n_batch):
    d = xu.shape[1]
    n_lat_rows = inp_l.shape[0]
    has_ctx = inp_c is not None
    rows = n_lat_rows + (inp_c.shape[0] if has_ctx else 0)
    tm = _tile(512, seq, *([inp_c.shape[0]] if has_ctx else []))
    n_lat_tiles = n_lat_rows // tm
    kin = inp_l.shape[1]
    in_specs = [pl.BlockSpec((tm, kin), lambda i: (jnp.minimum(i, n_lat_tiles - 1), 0))]
    args = [inp_l]
    if has_ctx:
        in_specs.append(pl.BlockSpec((tm, kin), lambda i: (jnp.maximum(i - n_lat_tiles, 0), 0)))
        args.append(inp_c)
    in_specs += [
        pl.BlockSpec((tm, d), lambda i: (i, 0)),
        pl.BlockSpec((None,) + mod.shape[1:], lambda i: (layer, 0, 0)),
        pl.BlockSpec((None, 1, d), lambda i: (layer, 0, 0)),
        pl.BlockSpec(w.shape, lambda i: (0, 0)),
    ]
    args += [xu, mod, g, w]
    kern = functools.partial(_proj_res_kernel, tm=tm, seq=seq, n_batch=n_batch,
                             n_lat_tiles=n_lat_tiles, has_ctx=has_ctx)
    return pl.pallas_call(
        kern,
        out_shape=jax.ShapeDtypeStruct((rows, d), F32),
        grid=(rows // tm,),
        in_specs=in_specs,
        out_specs=pl.BlockSpec((tm, d), lambda i: (i, 0)),
        compiler_params=_cparams(1),
        name="proj_res",
    )(*args)


def _mlp_kernel(x_ref, mod_ref, gpre_ref, gpost_ref, w1_ref, w2_ref, o_ref, u_ref,
                *, tm, seq, n_batch):
    i = pl.program_id(0)
    j = pl.program_id(1)
    d = x_ref.shape[1]
    r = jnp.minimum(lax.div(i * tm, seq), n_batch)

    @pl.when(j == 0)
    def _():
        u_ref[...] = _prenorm(x_ref[...], gpre_ref[...], mod_ref, r, 3, d).astype(BF16)

    h = jnp.maximum(jnp.dot(u_ref[...], w1_ref[...], preferred_element_type=F32), 0.0)
    part = jnp.dot((h * h).astype(BF16), w2_ref[...], preferred_element_type=F32)

    @pl.when(j == 0)
    def _():
        o_ref[...] = part

    @pl.when(j > 0)
    def _():
        o_ref[...] += part

    @pl.when(j == pl.num_programs(1) - 1)
    def _():
        gate = _mod_row(mod_ref, r, 5, d)
        o_ref[...] = x_ref[...] + gate * _rms(o_ref[...], gpost_ref[...], NORM_EPS)


def _mlp(xu, mod, gpre, gpost, w1, w2, *, layer, seq, n_batch, rows):
    d = xu.shape[1]
    dff = w1.shape[2]
    tm = _tile(512, seq, *([xu.shape[0] - n_batch * seq] if rows > n_batch * seq else []))
    tf = _tile(512, dff)
    kern = functools.partial(_mlp_kernel, tm=tm, seq=seq, n_batch=n_batch)
    return pl.pallas_call(
        kern,
        out_shape=jax.ShapeDtypeStruct((rows, d), F32),
        grid=(rows // tm, dff // tf),
        in_specs=[
            pl.BlockSpec((tm, d), lambda i, j: (i, 0)),
            pl.BlockSpec((None,) + mod.shape[1:], lambda i, j: (layer, 0, 0)),
            pl.BlockSpec((None, 1, d), lambda i, j: (layer, 0, 0)),
            pl.BlockSpec((None, 1, d), lambda i, j: (layer, 0, 0)),
            pl.BlockSpec((None, d, tf), lambda i, j: (layer, 0, j)),
            pl.BlockSpec((None, tf, d), lambda i, j: (layer, j, 0)),
        ],
        out_specs=pl.BlockSpec((tm, d), lambda i, j: (i, 0)),
        scratch_shapes=[pltpu.VMEM((tm, d), BF16)],
        compiler_params=_cparams(2),
        name="sq_relu_mlp",
    )(xu, mod, gpre, gpost, w1, w2)


def _dft_tables(n):
    idx = jnp.arange(n, dtype=jnp.int32)
    m = (idx[:, None] * idx[None, :]) % n
    ang = m.astype(F32) * (2.0 * math.pi / n)
    s = 1.0 / math.sqrt(n)
    return jnp.cos(ang) * s, jnp.sin(ang) * s


def _fourier_ch_kernel(x_ref, mod_ref, g_ref, cs_ref, a_ref, b_ref, *, tm, seq, n_batch):
    i = pl.program_id(0)
    d = x_ref.shape[1]
    cg = d // N_FFT_GROUPS
    r = jnp.minimum(lax.div(i * tm, seq), n_batch)
    u = _prenorm(x_ref[...], g_ref[...], mod_ref, r, 0, d).astype(BF16)
    for g in range(N_FFT_GROUPS):
        ab = jnp.dot(u[:, g * cg:(g + 1) * cg], cs_ref[...], preferred_element_type=F32)
        a_ref[:, g * cg:(g + 1) * cg] = ab[:, :cg].astype(BF16)
        b_ref[:, g * cg:(g + 1) * cg] = ab[:, cg:].astype(BF16)


def _fourier_channels(xu, mod, g, cs, *, layer, seq, n_batch, rows):
    d = xu.shape[1]
    tm = _tile(512, seq, *([xu.shape[0] - n_batch * seq] if rows > n_batch * seq else []))
    kern = functools.partial(_fourier_ch_kernel, tm=tm, seq=seq, n_batch=n_batch)
    return pl.pallas_call(
        kern,
        out_shape=[jax.ShapeDtypeStruct((rows, d), BF16)] * 2,
        grid=(rows // tm,),
        in_specs=[
            pl.BlockSpec((tm, d), lambda i: (i, 0)),
            pl.BlockSpec((None,) + mod.shape[1:], lambda i: (layer, 0, 0)),
            pl.BlockSpec((None, 1, d), lambda i: (layer, 0, 0)),
            pl.BlockSpec(cs.shape, lambda i: (0, 0)),
        ],
        out_specs=[pl.BlockSpec((tm, d), lambda i: (i, 0))] * 2,
        compiler_params=_cparams(1),
        name="fourier_channels",
    )(xu, mod, g, cs)


def _fourier_seq_kernel(tc_ref, tms_ref, a_ref, b_ref, o_ref, *, tmm):
    n = tc_ref.shape[0]
    for mi in range(n // tmm):
        rows = slice(mi * tmm, (mi + 1) * tmm)
        y = (jnp.dot(tc_ref[rows, :], a_ref[...], preferred_element_type=F32)
             + jnp.dot(tms_ref[rows, :], b_ref[...], preferred_element_type=F32))
        o_ref[rows, :] = y.astype(BF16)


def _fourier_seq(a, b, tc, tms, *, n_batch, length, row0):
    d = a.shape[1]
    tn = _tile(512, d)
    tmm = _tile(512, length)
    blk0 = row0 // length
    assert row0 % length == 0
    ab_spec = pl.BlockSpec((length, tn), lambda bi, ni: (blk0 + bi, ni))
    tab_spec = pl.BlockSpec((length, length), lambda bi, ni: (0, 0))
    return pl.pallas_call(
        functools.partial(_fourier_seq_kernel, tmm=tmm),
        out_shape=jax.ShapeDtypeStruct((n_batch * length, d), BF16),
        grid=(n_batch, d // tn),
        in_specs=[tab_spec, tab_spec, ab_spec, ab_spec],
        out_specs=pl.BlockSpec((length, tn), lambda bi, ni: (bi, ni)),
        compiler_params=_cparams(2),
        name="fourier_seq",
    )(tc, tms, a, b)


def _pool_kernel(x_ref, xp_ref, xn_ref, mod_ref, gpre_ref, gpost_ref, wp_ref, ps_ref, o_ref,
                 ext_ref, y_ref, *, tm, seq, ctx_len, n_batch, n_lat_rows):
    i = pl.program_id(0)
    d = x_ref.shape[1]
    cg = d // len(POOL_WINDOWS)
    row0 = i * tm
    is_lat = row0 < n_lat_rows
    slen = jnp.where(is_lat, seq, ctx_len)
    pos0 = jnp.where(is_lat, lax.rem(row0, seq), lax.rem(row0 - n_lat_rows, ctx_len))
    r = jnp.minimum(lax.div(row0, seq), n_batch)
    g_pre = gpre_ref[...]
    keep_prev = jnp.where(pos0 > 0, 1.0, 0.0).astype(F32)
    keep_next = jnp.where(pos0 + tm < slen, 1.0, 0.0).astype(F32)
    ext_ref[0:POOL_HALO, :] = _prenorm(xp_ref[...], g_pre, mod_ref, r, 0, d) * keep_prev
    ext_ref[POOL_HALO:POOL_HALO + tm, :] = _prenorm(x_ref[...], g_pre, mod_ref, r, 0, d)
    ext_ref[POOL_HALO + tm:, :] = _prenorm(xn_ref[...], g_pre, mod_ref, r, 0, d) * keep_next

    pos = pos0 + lax.broadcasted_iota(jnp.int32, (tm, 1), 0)
    for g, w in enumerate(POOL_WINDOWS):
        cols = slice(g * cg, (g + 1) * cg)
        acc = ext_ref[POOL_HALO - w // 2:POOL_HALO - w // 2 + tm, cols]
        for dlt in range(-w // 2 + 1, w // 2):
            acc = acc + ext_ref[POOL_HALO + dlt:POOL_HALO + dlt + tm, cols]
        lo = jnp.maximum(pos - w // 2, 0)
        hi = jnp.minimum(pos - w // 2 + w, slen)
        mean = acc / (hi - lo).astype(F32)
        diff = (mean - ext_ref[POOL_HALO:POOL_HALO + tm, cols]).astype(BF16)
        y_ref[:, cols] = jnp.dot(diff, wp_ref[g], preferred_element_type=F32) * ps_ref[:, cols]

    gate = _mod_row(mod_ref, r, 2, d)
    o_ref[...] = x_ref[...] + gate * _rms(y_ref[...], gpost_ref[...], NORM_EPS)


def _pool_mix(xu, mod, gpre, gpost, wp, ps, *, layer, ic, seq, ctx_len, n_batch, rows):
    d = xu.shape[1]
    n_lat_rows = n_batch * seq
    tm = _tile(256, seq, *([ctx_len] if rows > n_lat_rows else []))
    assert tm % POOL_HALO == 0 and POOL_HALO >= max(POOL_WINDOWS) // 2
    hb = tm // POOL_HALO
    last_hblk = xu.shape[0] // POOL_HALO - 1
    kern = functools.partial(_pool_kernel, tm=tm, seq=seq, ctx_len=ctx_len, n_batch=n_batch,
                             n_lat_rows=n_lat_rows)
    return pl.pallas_call(
        kern,
        out_shape=jax.ShapeDtypeStruct((rows, d), F32),
        grid=(rows // tm,),
        in_specs=[
            pl.BlockSpec((tm, d), lambda i: (i, 0)),
            pl.BlockSpec((POOL_HALO, d), lambda i: (jnp.maximum(i * hb - 1, 0), 0)),
            pl.BlockSpec((POOL_HALO, d), lambda i: (jnp.minimum((i + 1) * hb, last_hblk), 0)),
            pl.BlockSpec((None,) + mod.shape[1:], lambda i: (layer, 0, 0)),
            pl.BlockSpec((None, 1, d), lambda i: (layer, 0, 0)),
            pl.BlockSpec((None, 1, d), lambda i: (layer, 0, 0)),
            pl.BlockSpec((None,) + wp.shape[1:], lambda i: (ic, 0, 0, 0)),
            pl.BlockSpec((None, 1, d), lambda i: (ic, 0, 0)),
        ],
        out_specs=pl.BlockSpec((tm, d), lambda i: (i, 0)),
        scratch_shapes=[pltpu.VMEM((tm + 2 * POOL_HALO, d), F32), pltpu.VMEM((tm, d), F32)],
        compiler_params=_cparams(1),
        name="pool_mix",
    )(xu, xu, xu, mod, gpre, gpost, wp, ps)


def kernel(x, c, ctx, c_ctx, w_mod, b_mod, g_mix_pre, g_mix_post, g_mlp_pre, g_mlp_post,
           w_qkv, w_attn_out, lambda_q1, lambda_k1, lambda_q2, lambda_k2, g_subln,
           w_fourier_out, w_pool, pool_scale, w_mlp_in, w_mlp_out):
    n_batch, seq, d = x.shape
    ctx_len = ctx.shape[1]
    depth = w_mod.shape[0]
    n_lat_rows = n_batch * seq
    n_ctx_rows = n_batch * ctx_len
    n_rows = n_lat_rows + n_ctx_rows

    xu = jnp.concatenate([x.reshape(n_lat_rows, d), ctx.reshape(n_ctx_rows, d)], axis=0)
    n_cond = -(-(n_batch + 1) // 8) * 8
    cond = jnp.concatenate(
        [c, c_ctx[None, :], jnp.zeros((n_cond - n_batch - 1, d), F32)], axis=0)
    mod = _mod_all(cond, w_mod, b_mod)

    w_qkv_b = w_qkv.astype(BF16)
    w_o_b = w_attn_out.astype(BF16)
    w_f_b = w_fourier_out.astype(BF16)
    w_p_b = w_pool.astype(BF16)
    w1_b = w_mlp_in.astype(BF16)
    w2_b = w_mlp_out.astype(BF16)

    row3 = lambda t: t.reshape(t.shape[0], 1, t.shape[1])
    g_mix_pre, g_mix_post, g_mlp_pre, g_mlp_post = map(
        row3, (g_mix_pre, g_mix_post, g_mlp_pre, g_mlp_post))
    lambda_q1, lambda_k1, lambda_q2, lambda_k2, g_subln, pool_scale = map(
        row3, (lambda_q1, lambda_k1, lambda_q2, lambda_k2, g_subln, pool_scale))

    rope = _rope_tables(seq)
    common = dict(seq=seq, n_batch=n_batch)

    ia = ib = ic = 0
    for i in range(depth):
        last = i == depth - 1
        kind = i % N_MIXERS
        out_rows = n_lat_rows if last else n_rows
        if kind == 0:
            lam_init = 0.8 - 0.6 * math.exp(-0.3 * i)
            qkv = _qkv_proj(xu, mod, g_mix_pre, w_qkv_b[ia], rope, layer=i,
                            n_lat_rows=n_lat_rows, **common)
            lam_params = (lambda_q1, lambda_k1, lambda_q2, lambda_k2)
            akw = dict(ia=ia, lam_init=lam_init, n_batch=n_batch)
            o_l = _attention(qkv, lam_params, g_subln, q_len=seq, q_row0=0,
                             segs=[(0, seq), (n_lat_rows, ctx_len)], **akw)
            o_c = None if last else _attention(
                qkv, lam_params, g_subln, q_len=ctx_len, q_row0=n_lat_rows,
                segs=[(n_lat_rows, ctx_len)], **akw)
            xu = _proj_res(o_l, o_c, xu, mod, g_mix_post, w_o_b[ia], layer=i, **common)
            ia += 1
        elif kind == 1:
            cg = d // N_FFT_GROUPS
            cc, sc = _dft_tables(cg)
            cs = jnp.concatenate([cc, sc], axis=1).astype(BF16)
            a, b = _fourier_channels(xu, mod, g_mix_pre, cs, layer=i, rows=out_rows, **common)
            tc, ts = _dft_tables(seq)
            y_l = _fourier_seq(a, b, tc.astype(BF16), (-ts).astype(BF16),
                               n_batch=n_batch, length=seq, row0=0)
            y_c = None
            if not last:
                tcc, tsc = _dft_tables(ctx_len)
                y_c = _fourier_seq(a, b, tcc.astype(BF16), (-tsc).astype(BF16),
                                   n_batch=n_batch, length=ctx_len, row0=n_lat_rows)
            xu = _proj_res(y_l, y_c, xu, mod, g_mix_post, w_f_b[ib], layer=i, **common)
            ib += 1
        else:
            xu = _pool_mix(xu, mod, g_mix_pre, g_mix_post, w_p_b, pool_scale, layer=i, ic=ic,
                           ctx_len=ctx_len, rows=out_rows, **common)
            ic += 1
        xu = _mlp(xu, mod, g_mlp_pre, g_mlp_post, w1_b, w2_b, layer=i, rows=out_rows, **common)
    return xu[:n_lat_rows].reshape(n_batch, seq, d)
```

```python
import functools
import math

import jax
import jax.numpy as jnp
from jax import lax
from jax.experimental import pallas as pl
from jax.experimental.pallas import tpu as pltpu

F32 = jnp.float32
BF16 = jnp.bfloat16

GRID_W = 64
N_MIXERS = 3
N_HEADS = 16
HEAD_DIM = 64
V_HEAD_DIM = 2 * HEAD_DIM
ROPE_BASE = 10000.0
AXIS_ROT = HEAD_DIM // 2
SUBLN_EPS = 1e-5
N_FFT_GROUPS = 4
POOL_WINDOWS = (2, 4, 8, 16)
NORM_EPS = 1e-6
N_MOD = 6
POOL_HALO = 8

LANES = 128
V7X_VMEM_BYTES = 64 * 1024 * 1024
VMEM_LIMIT = 56 * 1024 * 1024


def _cparams(n_axes):
    return pltpu.CompilerParams(
        dimension_semantics=("arbitrary",) * n_axes, vmem_limit_bytes=VMEM_LIMIT)


def _tile(pref, *counts):
    t = min(pref, *counts)
    while t > 8 and any(c % t for c in counts):
        t -= 8
    assert t >= 8 and all(c % t == 0 for c in counts), (pref, counts)
    return t


def _rms(x, g, eps):
    return x * lax.rsqrt(jnp.mean(x * x, axis=-1, keepdims=True) + eps) * g


def _mod_row(mod_ref, r, k, d):
    return mod_ref[pl.ds(r, 1), k * d:(k + 1) * d]


def _prenorm(x, g, mod_ref, r, k_shift, d):
    sh = _mod_row(mod_ref, r, k_shift, d)
    sc = _mod_row(mod_ref, r, k_shift + 1, d)
    return _rms(x, g, NORM_EPS) * (1.0 + sc) + sh


def _mod_kernel(cond_ref, w_ref, b_ref, o_ref):
    s = jax.nn.silu(cond_ref[...]).astype(BF16)
    o_ref[...] = jnp.dot(s, w_ref[...].astype(BF16), preferred_element_type=F32) + b_ref[...]


def _mod_all(cond, w_mod, b_mod):
    depth, d, n = w_mod.shape
    rows = cond.shape[0]
    tn = _tile(1024, n)
    return pl.pallas_call(
        _mod_kernel,
        out_shape=jax.ShapeDtypeStruct((depth, rows, n), F32),
        grid=(depth, n // tn),
        in_specs=[
            pl.BlockSpec((rows, d), lambda l, j: (0, 0)),
            pl.BlockSpec((None, d, tn), lambda l, j: (l, 0, j)),
            pl.BlockSpec((None, 1, tn), lambda l, j: (l, 0, j)),
        ],
        out_specs=pl.BlockSpec((None, rows, tn), lambda l, j: (l, 0, j)),
        compiler_params=_cparams(2),
        name="ada_mod",
    )(cond, w_mod, b_mod.reshape(depth, 1, n))


def _qkv_kernel(x_ref, mod_ref, g_ref, w_ref, tab_ref, o_ref, u_ref, *, tm, tn, seq, n_batch):
    i = pl.program_id(0)
    d = x_ref.shape[1]

    @pl.when(pl.program_id(1) == 0)
    def _():
        r = jnp.minimum(lax.div(i * tm, seq), n_batch)
        u_ref[...] = _prenorm(x_ref[...], g_ref[...], mod_ref, r, 0, d).astype(BF16)

    rb = _tile(256, tm)
    for b in range(tm // rb):
        rows = slice(b * rb, (b + 1) * rb)
        y = jnp.dot(u_ref[rows, :], w_ref[...], preferred_element_type=F32)
        cos = tab_ref[0, rows, :]
        sa = tab_ref[1, rows, :]
        sb = tab_ref[2, rows, :]
        for c in range(tn // LANES):
            yc = y[:, c * LANES:(c + 1) * LANES]
            rot = (yc * cos + pltpu.roll(yc, LANES - AXIS_ROT // 2, 1) * sa
                   + pltpu.roll(yc, AXIS_ROT // 2, 1) * sb)
            o_ref[c, rows, :] = rot.astype(BF16)


def _qkv_proj(xu, mod, g, w, rope, *, layer, seq, n_batch, n_lat_rows):
    rows, d = xu.shape
    n = w.shape[1]
    tm = _tile(1024, seq, rows - n_lat_rows)
    tn = _tile(512, n // 3)
    assert tn % LANES == 0
    lat_tiles_per_seq = seq // tm
    n_lat_tiles = n_lat_rows // tm
    n_q_j = n // 3 // tn

    def tab_index(i, j):
        lat = jnp.where(j < n_q_j, 0, jnp.where(j < 2 * n_q_j, 1, 2))
        ctx = jnp.where(j < n_q_j, 3, 2)
        return (jnp.where(i < n_lat_tiles, lat, ctx), 0, i % lat_tiles_per_seq, 0)

    kern = functools.partial(_qkv_kernel, tm=tm, tn=tn, seq=seq, n_batch=n_batch)
    return pl.pallas_call(
        kern,
        out_shape=jax.ShapeDtypeStruct((n // LANES, rows, LANES), BF16),
        grid=(rows // tm, n // tn),
        in_specs=[
            pl.BlockSpec((tm, d), lambda i, j: (i, 0)),
            pl.BlockSpec((None,) + mod.shape[1:], lambda i, j: (layer, 0, 0)),
            pl.BlockSpec((None, 1, d), lambda i, j: (layer, 0, 0)),
            pl.BlockSpec((d, tn), lambda i, j: (0, j)),
            pl.BlockSpec((None, 3, tm, LANES), tab_index),
        ],
        out_specs=pl.BlockSpec((tn // LANES, tm, LANES), lambda i, j: (j, i, 0)),
        scratch_shapes=[pltpu.VMEM((tm, d), BF16)],
        compiler_params=_cparams(2),
        name="qkv_proj",
    )(xu, mod, g, w, rope)


def _rope_tables(seq):
    rows = seq // GRID_W
    row = jnp.repeat(jnp.arange(rows), GRID_W).astype(F32)
    col = jnp.tile(jnp.arange(GRID_W), rows).astype(F32)
    n_freq = AXIS_ROT // 2
    inv = 1.0 / (ROPE_BASE ** (jnp.arange(n_freq, dtype=F32) / n_freq))
    ang_r = row[:, None] * inv
    ang_c = col[:, None] * inv
    cr, sr, cc, sc = jnp.cos(ang_r), jnp.sin(ang_r), jnp.cos(ang_c), jnp.sin(ang_c)
    z = jnp.zeros_like(sr)
    reps = LANES // HEAD_DIM
    cos = jnp.tile(jnp.concatenate([cr, cr, cc, cc], axis=1), (1, reps))
    sa = jnp.tile(jnp.concatenate([-sr, z, -sc, z], axis=1), (1, reps))
    sb = jnp.tile(jnp.concatenate([z, sr, z, sc], axis=1), (1, reps))
    rot = jnp.stack([cos, sa, sb])
    ident = jnp.stack([jnp.ones_like(cos), jnp.zeros_like(cos), jnp.zeros_like(cos)])
    qs = 1.0 / math.sqrt(HEAD_DIM)
    return jnp.stack([rot * qs, rot, ident, ident * qs])


def _attn_kernel(*refs, n_seg, lam_init, rb):
    q_ref = refs[0]
    seg_refs = refs[1:1 + 2 * n_seg]
    lq1_ref, lk1_ref, lq2_ref, lk2_ref, gs_ref, o_ref, kk_ref, va_ref = refs[1 + 2 * n_seg:]

    @pl.when(pl.program_id(2) == 0)
    def _():
        off = 0
        for s in range(n_seg):
            n = seg_refs[2 * s].shape[0]
            kk_ref[off:off + n, :] = seg_refs[2 * s][...]
            va_ref[off:off + n, 0:V_HEAD_DIM] = seg_refs[2 * s + 1][...]
            off += n
        va_ref[:, V_HEAD_DIM:] = jnp.ones((va_ref.shape[0], V_HEAD_DIM), BF16)

    lam = _lambda(lq1_ref, lk1_ref, lq2_ref, lk2_ref, lam_init)
    gs = gs_ref[...]
    lane = lax.broadcasted_iota(jnp.int32, (rb, LANES), 1)
    n_blocks = q_ref.shape[0] // rb

    def scores(r):
        q = q_ref[r * rb:(r + 1) * rb, :]
        qq = jnp.concatenate([jnp.where(lane < HEAD_DIM, q, jnp.zeros_like(q)),
                              jnp.where(lane >= HEAD_DIM, q, jnp.zeros_like(q))], axis=0)
        return lax.dot_general(qq, kk_ref[...], (((1,), (1,)), ((), ())),
                               preferred_element_type=F32)

    def finish(r, s):
        p = jnp.exp(s - jnp.max(s, axis=-1, keepdims=True)).astype(BF16)
        acc = jnp.dot(p, va_ref[...], preferred_element_type=F32)
        ratio = acc[:, :V_HEAD_DIM] / acc[:, V_HEAD_DIM:]
        o = ratio[:rb] - lam * ratio[rb:]
        o = _rms(o, gs, SUBLN_EPS) * (1.0 - lam_init)
        o_ref[r * rb:(r + 1) * rb, :] = o.astype(BF16)

    s_next = scores(0)
    for r in range(n_blocks):
        s_cur = s_next
        if r + 1 < n_blocks:
            s_next = scores(r + 1)
        finish(r, s_cur)


def _attention(qkv, lam_params, gs, *, ia, lam_init, n_batch, q_len, q_row0, segs):
    n_cols = qkv.shape[0]
    n_heads = n_cols // 3
    tq = _tile(512, q_len)
    n_qt = q_len // tq
    nk = sum(n for _, n in segs)
    q_blk0 = q_row0 // tq
    in_specs = [pl.BlockSpec((None, tq, LANES),
                             lambda b, h, qi: (h, q_blk0 + b * n_qt + qi, 0))]
    args = [qkv]
    for row0, n in segs:
        blk0 = row0 // n
        assert row0 % n == 0
        in_specs.append(pl.BlockSpec(
            (None, n, LANES), lambda b, h, qi, blk0=blk0: (n_heads + h, blk0 + b, 0)))
        in_specs.append(pl.BlockSpec(
            (None, n, LANES), lambda b, h, qi, blk0=blk0: (2 * n_heads + h, blk0 + b, 0)))
        args += [qkv, qkv]
    for p in lam_params:
        in_specs.append(pl.BlockSpec((None, 1, HEAD_DIM), lambda b, h, qi: (ia, 0, 0)))
        args.append(p)
    in_specs.append(pl.BlockSpec((None, 1, V_HEAD_DIM), lambda b, h, qi: (ia, 0, 0)))
    args.append(gs)
    return pl.pallas_call(
        functools.partial(_attn_kernel, n_seg=len(segs), lam_init=lam_init, rb=_tile(128, tq)),
        out_shape=jax.ShapeDtypeStruct((n_batch * q_len, n_heads * V_HEAD_DIM), BF16),
        grid=(n_batch, n_heads, n_qt),
        in_specs=in_specs,
        out_specs=pl.BlockSpec((tq, V_HEAD_DIM), lambda b, h, qi: (b * n_qt + qi, h)),
        scratch_shapes=[pltpu.VMEM((nk, LANES), BF16), pltpu.VMEM((nk, 2 * V_HEAD_DIM), BF16)],
        compiler_params=_cparams(3),
        name="diff_attn",
    )(*args)


def _lambda(lq1_ref, lk1_ref, lq2_ref, lk2_ref, lam_init):
    return (jnp.exp(jnp.sum(lq1_ref[...] * lk1_ref[...], axis=-1, keepdims=True))
            - jnp.exp(jnp.sum(lq2_ref[...] * lk2_ref[...], axis=-1, keepdims=True)) + lam_init)


def _proj_res_kernel(*refs, tm, seq, n_batch, n_lat_tiles, has_ctx):
    if has_ctx:
        inl_ref, inc_ref, x_ref, mod_ref, g_ref, w_ref, o_ref = refs
    else:
        inl_ref, x_ref, mod_ref, g_ref, w_ref, o_ref = refs
    i = pl.program_id(0)
    d = x_ref.shape[1]

    def body(inp):
        y = jnp.dot(inp, w_ref[...], preferred_element_type=F32)
        r = jnp.minimum(lax.div(i * tm, seq), n_batch)
        gate = _mod_row(mod_ref, r, 2, d)
        o_ref[...] = x_ref[...] + gate * _rms(y, g_ref[...], NORM_EPS)

    if has_ctx:
        @pl.when(i < n_lat_tiles)
        def _():
            body(inl_ref[...])

        @pl.when(i >= n_lat_tiles)
        def _():
            body(inc_ref[...])
    else:
        body(inl_ref[...])


def _proj_res(inp_l, inp_c, xu, mod, g, w, *, layer, seq, n_batch):
    d = xu.shape[1]
    n_lat_rows = inp_l.shape[0]
    has_ctx = inp_c is not None
    rows = n_lat_rows + (inp_c.shape[0] if has_ctx else 0)
    tm = _tile(512, seq, *([inp_c.shape[0]] if has_ctx else []))
    n_lat_tiles = n_lat_rows // tm
    kin = inp_l.shape[1]
    in_specs = [pl.BlockSpec((tm, kin), lambda i: (jnp.minimum(i, n_lat_tiles - 1), 0))]
    args = [inp_l]
    if has_ctx:
        in_specs.append(pl.BlockSpec((tm, kin), lambda i: (jnp.maximum(i - n_lat_tiles, 0), 0)))
        args.append(inp_c)
    in_specs += [
        pl.BlockSpec((tm, d), lambda i: (i, 0)),
        pl.BlockSpec((None,) + mod.shape[1:], lambda i: (layer, 0, 0)),
        pl.BlockSpec((None, 1, d), lambda i: (layer, 0, 0)),
        pl.BlockSpec(w.shape, lambda i: (0, 0)),
    ]
    args += [xu, mod, g, w]
    kern = functools.partial(_proj_res_kernel, tm=tm, seq=seq, n_batch=n_batch,
                             n_lat_tiles=n_lat_tiles, has_ctx=has_ctx)
    return pl.pallas_call(
        kern,
        out_shape=jax.ShapeDtypeStruct((rows, d), F32),
        grid=(rows // tm,),
        in_specs=in_specs,
        out_specs=pl.BlockSpec((tm, d), lambda i: (i, 0)),
        compiler_params=_cparams(1),
        name="proj_res",
    )(*args)


def _mlp_kernel(x_ref, mod_ref, gpre_ref, gpost_ref, w1_ref, w2_ref, o_ref, u_ref, h_ref, y_ref,
                *, tm, seq, n_batch):
    i = pl.program_id(0)
    j = pl.program_id(1)
    d = x_ref.shape[1]
    n1, _, tf = h_ref.shape
    n2, _, tn = y_ref.shape
    r = jnp.minimum(lax.div(i * tm, seq), n_batch)

    @pl.when(j == 0)
    def _():
        u_ref[...] = _prenorm(x_ref[...], gpre_ref[...], mod_ref, r, 3, d).astype(BF16)

    @pl.when(j < n1)
    def _():
        h = jnp.maximum(jnp.dot(u_ref[...], w1_ref[...], preferred_element_type=F32), 0.0)
        h_ref[j] = (h * h).astype(BF16)

    @pl.when(j >= n1)
    def _():
        acc = jnp.dot(h_ref[0], w2_ref[0:tf, :], preferred_element_type=F32)
        for k in range(1, n1):
            acc += jnp.dot(h_ref[k], w2_ref[k * tf:(k + 1) * tf, :], preferred_element_type=F32)
        y_ref[j - n1] = acc

    @pl.when(j == n1 + n2 - 1)
    def _():
        ss = jnp.sum(y_ref[0] * y_ref[0], axis=-1, keepdims=True)
        for k in range(1, n2):
            ss += jnp.sum(y_ref[k] * y_ref[k], axis=-1, keepdims=True)
        inv = lax.rsqrt(ss / d + NORM_EPS)
        for k in range(n2):
            cols = slice(k * tn, (k + 1) * tn)
            gate = mod_ref[pl.ds(r, 1), 5 * d + k * tn:5 * d + (k + 1) * tn]
            o_ref[:, cols] = x_ref[:, cols] + gate * (y_ref[k] * inv * gpost_ref[:, cols])


def _mlp(xu, mod, gpre, gpost, w1, w2, *, layer, seq, n_batch, rows):
    d = xu.shape[1]
    dff = w1.shape[2]
    tm = _tile(512, seq, *([xu.shape[0] - n_batch * seq] if rows > n_batch * seq else []))
    tf = _tile(1024, dff)
    tn = _tile(256, d)
    n1 = dff // tf
    n2 = d // tn
    kern = functools.partial(_mlp_kernel, tm=tm, seq=seq, n_batch=n_batch)
    return pl.pallas_call(
        kern,
        out_shape=jax.ShapeDtypeStruct((rows, d), F32),
        grid=(rows // tm, n1 + n2),
        in_specs=[
            pl.BlockSpec((tm, d), lambda i, j: (i, 0)),
            pl.BlockSpec((None,) + mod.shape[1:], lambda i, j: (layer, 0, 0)),
            pl.BlockSpec((None, 1, d), lambda i, j: (layer, 0, 0)),
            pl.BlockSpec((None, 1, d), lambda i, j: (layer, 0, 0)),
            pl.BlockSpec((None, d, tf), lambda i, j: (layer, 0, jnp.minimum(j, n1 - 1))),
            pl.BlockSpec((None, dff, tn), lambda i, j: (layer, 0, jnp.maximum(j - n1, 0))),
        ],
        out_specs=pl.BlockSpec((tm, d), lambda i, j: (i, 0)),
        scratch_shapes=[pltpu.VMEM((tm, d), BF16), pltpu.VMEM((n1, tm, tf), BF16),
                        pltpu.VMEM((n2, tm, tn), F32)],
        compiler_params=_cparams(2),
        name="sq_relu_mlp",
    )(xu, mod, gpre, gpost, w1, w2)


def _dft_tables(n):
    idx = jnp.arange(n, dtype=jnp.int32)
    m = (idx[:, None] * idx[None, :]) % n
    ang = m.astype(F32) * (2.0 * math.pi / n)
    s = 1.0 / math.sqrt(n)
    return jnp.cos(ang) * s, jnp.sin(ang) * s


def _fourier_ch_kernel(x_ref, mod_ref, g_ref, cs_ref, a_ref, b_ref, *, tm, seq, n_batch):
    i = pl.program_id(0)
    d = x_ref.shape[1]
    cg = d // N_FFT_GROUPS
    r = jnp.minimum(lax.div(i * tm, seq), n_batch)
    u = _prenorm(x_ref[...], g_ref[...], mod_ref, r, 0, d).astype(BF16)
    for g in range(N_FFT_GROUPS):
        ab = jnp.dot(u[:, g * cg:(g + 1) * cg], cs_ref[...], preferred_element_type=F32)
        a_ref[:, g * cg:(g + 1) * cg] = ab[:, :cg].astype(BF16)
        b_ref[:, g * cg:(g + 1) * cg] = ab[:, cg:].astype(BF16)


def _fourier_channels(xu, mod, g, cs, *, layer, seq, n_batch, rows):
    d = xu.shape[1]
    tm = _tile(512, seq, *([xu.shape[0] - n_batch * seq] if rows > n_batch * seq else []))
    kern = functools.partial(_fourier_ch_kernel, tm=tm, seq=seq, n_batch=n_batch)
    return pl.pallas_call(
        kern,
        out_shape=[jax.ShapeDtypeStruct((rows, d), BF16)] * 2,
        grid=(rows // tm,),
        in_specs=[
            pl.BlockSpec((tm, d), lambda i: (i, 0)),
            pl.BlockSpec((None,) + mod.shape[1:], lambda i: (layer, 0, 0)),
            pl.BlockSpec((None, 1, d), lambda i: (layer, 0, 0)),
            pl.BlockSpec(cs.shape, lambda i: (0, 0)),
        ],
        out_specs=[pl.BlockSpec((tm, d), lambda i: (i, 0))] * 2,
        compiler_params=_cparams(1),
        name="fourier_channels",
    )(xu, mod, g, cs)


def _fourier_seq_kernel(tc_ref, tms_ref, a_ref, b_ref, o_ref, *, tmm):
    n = tc_ref.shape[0]
    for mi in range(n // tmm):
        rows = slice(mi * tmm, (mi + 1) * tmm)
        y = (jnp.dot(tc_ref[rows, :], a_ref[...], preferred_element_type=F32)
             + jnp.dot(tms_ref[rows, :], b_ref[...], preferred_element_type=F32))
        o_ref[rows, :] = y.astype(BF16)


def _fourier_seq(a, b, tc, tms, *, n_batch, length, row0):
    d = a.shape[1]
    tn = _tile(512, d)
    tmm = _tile(512, length)
    blk0 = row0 // length
    assert row0 % length == 0
    ab_spec = pl.BlockSpec((length, tn), lambda bi, ni: (blk0 + bi, ni))
    tab_spec = pl.BlockSpec((length, length), lambda bi, ni: (0, 0))
    return pl.pallas_call(
        functools.partial(_fourier_seq_kernel, tmm=tmm),
        out_shape=jax.ShapeDtypeStruct((n_batch * length, d), BF16),
        grid=(n_batch, d // tn),
        in_specs=[tab_spec, tab_spec, ab_spec, ab_spec],
        out_specs=pl.BlockSpec((length, tn), lambda bi, ni: (bi, ni)),
        compiler_params=_cparams(2),
        name="fourier_seq",
    )(tc, tms, a, b)


def _pool_kernel(x_ref, xp_ref, xn_ref, mod_ref, gpre_ref, gpost_ref, wp_ref, ps_ref, o_ref,
                 ext_ref, y_ref, *, tm, seq, ctx_len, n_batch, n_lat_rows):
    i = pl.program_id(0)
    d = x_ref.shape[1]
    cg = d // len(POOL_WINDOWS)
    row0 = i * tm
    is_lat = row0 < n_lat_rows
    slen = jnp.where(is_lat, seq, ctx_len)
    pos0 = jnp.where(is_lat, lax.rem(row0, seq), lax.rem(row0 - n_lat_rows, ctx_len))
    r = jnp.minimum(lax.div(row0, seq), n_batch)
    g_pre = gpre_ref[...]
    keep_prev = jnp.where(pos0 > 0, 1.0, 0.0).astype(F32)
    keep_next = jnp.where(pos0 + tm < slen, 1.0, 0.0).astype(F32)
    ext_ref[0:POOL_HALO, :] = _prenorm(xp_ref[...], g_pre, mod_ref, r, 0, d) * keep_prev
    ext_ref[POOL_HALO:POOL_HALO + tm, :] = _prenorm(x_ref[...], g_pre, mod_ref, r, 0, d)
    ext_ref[POOL_HALO + tm:, :] = _prenorm(xn_ref[...], g_pre, mod_ref, r, 0, d) * keep_next

    pos = pos0 + lax.broadcasted_iota(jnp.int32, (tm, 1), 0)
    for g, w in enumerate(POOL_WINDOWS):
        cols = slice(g * cg, (g + 1) * cg)
        acc = ext_ref[POOL_HALO - w // 2:POOL_HALO - w // 2 + tm, cols]
        for dlt in range(-w // 2 + 1, w // 2):
            acc = acc + ext_ref[POOL_HALO + dlt:POOL_HALO + dlt + tm, cols]
        lo = jnp.maximum(pos - w // 2, 0)
        hi = jnp.minimum(pos - w // 2 + w, slen)
        mean = acc / (hi - lo).astype(F32)
        diff = (mean - ext_ref[POOL_HALO:POOL_HALO + tm, cols]).astype(BF16)
        y_ref[:, cols] = jnp.dot(diff, wp_ref[g], preferred_element_type=F32) * ps_ref[:, cols]

    gate = _mod_row(mod_ref, r, 2, d)
    o_ref[...] = x_ref[...] + gate * _rms(y_ref[...], gpost_ref[...], NORM_EPS)


def _pool_mix(xu, mod, gpre, gpost, wp, ps, *, layer, ic, seq, ctx_len, n_batch, rows):
    d = xu.shape[1]
    n_lat_rows = n_batch * seq
    tm = _tile(256, seq, *([ctx_len] if rows > n_lat_rows else []))
    assert tm % POOL_HALO == 0 and POOL_HALO >= max(POOL_WINDOWS) // 2
    hb = tm // POOL_HALO
    last_hblk = xu.shape[0] // POOL_HALO - 1
    kern = functools.partial(_pool_kernel, tm=tm, seq=seq, ctx_len=ctx_len, n_batch=n_batch,
                             n_lat_rows=n_lat_rows)
    return pl.pallas_call(
        kern,
        out_shape=jax.ShapeDtypeStruct((rows, d), F32),
        grid=(rows // tm,),
        in_specs=[
            pl.BlockSpec((tm, d), lambda i: (i, 0)),
            pl.BlockSpec((POOL_HALO, d), lambda i: (jnp.maximum(i * hb - 1, 0), 0)),
            pl.BlockSpec((POOL_HALO, d), lambda i: (jnp.minimum((i + 1) * hb, last_hblk), 0)),
            pl.BlockSpec((None,) + mod.shape[1:], lambda i: (layer, 0, 0)),
            pl.BlockSpec((None, 1, d), lambda i: (layer, 0, 0)),
            pl.BlockSpec((None, 1, d), lambda i: (layer, 0, 0)),
            pl.BlockSpec((None,) + wp.shape[1:], lambda i: (ic, 0, 0, 0)),
            pl.BlockSpec((None, 1, d), lambda i: (ic, 0, 0)),
        ],
        out_specs=pl.BlockSpec((tm, d), lambda i: (i, 0)),
        scratch_shapes=[pltpu.VMEM((tm + 2 * POOL_HALO, d), F32), pltpu.VMEM((tm, d), F32)],
        compiler_params=_cparams(1),
        name="pool_mix",
    )(xu, xu, xu, mod, gpre, gpost, wp, ps)


def kernel(x, c, ctx, c_ctx, w_mod, b_mod, g_mix_pre, g_mix_post, g_mlp_pre, g_mlp_post,
           w_qkv, w_attn_out, lambda_q1, lambda_k1, lambda_q2, lambda_k2, g_subln,
           w_fourier_out, w_pool, pool_scale, w_mlp_in, w_mlp_out):
    n_batch, seq, d = x.shape
    ctx_len = ctx.shape[1]
    depth = w_mod.shape[0]
    n_lat_rows = n_batch * seq
    n_ctx_rows = n_batch * ctx_len
    n_rows = n_lat_rows + n_ctx_rows

    xu = jnp.concatenate([x.reshape(n_lat_rows, d), ctx.reshape(n_ctx_rows, d)], axis=0)
    n_cond = -(-(n_batch + 1) // 8) * 8
    cond = jnp.concatenate(
        [c, c_ctx[None, :], jnp.zeros((n_cond - n_batch - 1, d), F32)], axis=0)
    mod = _mod_all(cond, w_mod, b_mod)

    w_qkv_b = w_qkv.astype(BF16)
    w_o_b = w_attn_out.astype(BF16)
    w_f_b = w_fourier_out.astype(BF16)
    w_p_b = w_pool.astype(BF16)
    w1_b = w_mlp_in.astype(BF16)
    w2_b = w_mlp_out.astype(BF16)

    row3 = lambda t: t.reshape(t.shape[0], 1, t.shape[1])
    g_mix_pre, g_mix_post, g_mlp_pre, g_mlp_post = map(
        row3, (g_mix_pre, g_mix_post, g_mlp_pre, g_mlp_post))
    lambda_q1, lambda_k1, lambda_q2, lambda_k2, g_subln, pool_scale = map(
        row3, (lambda_q1, lambda_k1, lambda_q2, lambda_k2, g_subln, pool_scale))

    rope = _rope_tables(seq)
    common = dict(seq=seq, n_batch=n_batch)

    ia = ib = ic = 0
    for i in range(depth):
        last = i == depth - 1
        kind = i % N_MIXERS
        out_rows = n_lat_rows if last else n_rows
        if kind == 0:
            lam_init = 0.8 - 0.6 * math.exp(-0.3 * i)
            qkv = _qkv_proj(xu, mod, g_mix_pre, w_qkv_b[ia], rope, layer=i,
                            n_lat_rows=n_lat_rows, **common)
            lam_params = (lambda_q1, lambda_k1, lambda_q2, lambda_k2)
            akw = dict(ia=ia, lam_init=lam_init, n_batch=n_batch)
            o_l = _attention(qkv, lam_params, g_subln, q_len=seq, q_row0=0,
                             segs=[(0, seq), (n_lat_rows, ctx_len)], **akw)
            o_c = None if last else _attention(
                qkv, lam_params, g_subln, q_len=ctx_len, q_row0=n_lat_rows,
                segs=[(n_lat_rows, ctx_len)], **akw)
            xu = _proj_res(o_l, o_c, xu, mod, g_mix_post, w_o_b[ia], layer=i, **common)
            ia += 1
        elif kind == 1:
            cg = d // N_FFT_GROUPS
            cc, sc = _dft_tables(cg)
            cs = jnp.concatenate([cc, sc], axis=1).astype(BF16)
            a, b = _fourier_channels(xu, mod, g_mix_pre, cs, layer=i, rows=out_rows, **common)
            tc, ts = _dft_tables(seq)
            y_l = _fourier_seq(a, b, tc.astype(BF16), (-ts).astype(BF16),
                               n_batch=n_batch, length=seq, row0=0)
            y_c = None
            if not last:
                tcc, tsc = _dft_tables(ctx_len)
                y_c = _fourier_seq(a, b, tcc.astype(BF16), (-tsc).astype(BF16),
                                   n_batch=n_batch, length=ctx_len, row0=n_lat_rows)
            xu = _proj_res(y_l, y_c, xu, mod, g_mix_post, w_f_b[ib], layer=i, **common)
            ib += 1
        else:
            xu = _pool_mix(xu, mod, g_mix_pre, g_mix_post, w_p_b, pool_scale, layer=i, ic=ic,
                           ctx_len=ctx_len, rows=out_rows, **common)
            ic += 1
        xu = _mlp(xu, mod, g_mlp_pre, g_mlp_post, w1_b, w2_b, layer=i, rows=out_rows, **common)
    return xu[:n_lat_rows].reshape(n_batch, seq, d)
```

```python
import functools
import math

import jax
import jax.numpy as jnp
from jax import lax
from jax.experimental import pallas as pl
from jax.experimental.pallas import tpu as pltpu

F32 = jnp.float32
BF16 = jnp.bfloat16

GRID_W = 64
N_MIXERS = 3
N_HEADS = 16
HEAD_DIM = 64
V_HEAD_DIM = 2 * HEAD_DIM
ROPE_BASE = 10000.0
AXIS_ROT = HEAD_DIM // 2
SUBLN_EPS = 1e-5
N_FFT_GROUPS = 4
POOL_WINDOWS = (2, 4, 8, 16)
NORM_EPS = 1e-6
N_MOD = 6
POOL_HALO = 8

LANES = 128
V7X_VMEM_BYTES = 64 * 1024 * 1024
VMEM_LIMIT = 56 * 1024 * 1024


def _cparams(n_axes):
    return pltpu.CompilerParams(
        dimension_semantics=("arbitrary",) * n_axes, vmem_limit_bytes=VMEM_LIMIT)


def _tile(pref, *counts):
    t = min(pref, *counts)
    while t > 8 and any(c % t for c in counts):
        t -= 8
    assert t >= 8 and all(c % t == 0 for c in counts), (pref, counts)
    return t


def _rms(x, g, eps):
    return x * lax.rsqrt(jnp.mean(x * x, axis=-1, keepdims=True) + eps) * g


def _mod_row(mod_ref, r, k, d):
    return mod_ref[pl.ds(r, 1), k * d:(k + 1) * d]


def _prenorm(x, g, mod_ref, r, k_shift, d):
    sh = _mod_row(mod_ref, r, k_shift, d)
    sc = _mod_row(mod_ref, r, k_shift + 1, d)
    return _rms(x, g, NORM_EPS) * (1.0 + sc) + sh


def _mod_kernel(cond_ref, w_ref, b_ref, o_ref):
    s = jax.nn.silu(cond_ref[...]).astype(BF16)
    o_ref[...] = jnp.dot(s, w_ref[...].astype(BF16), preferred_element_type=F32) + b_ref[...]


def _mod_all(cond, w_mod, b_mod):
    depth, d, n = w_mod.shape
    rows = cond.shape[0]
    tn = _tile(1024, n)
    return pl.pallas_call(
        _mod_kernel,
        out_shape=jax.ShapeDtypeStruct((depth, rows, n), F32),
        grid=(depth, n // tn),
        in_specs=[
            pl.BlockSpec((rows, d), lambda l, j: (0, 0)),
            pl.BlockSpec((None, d, tn), lambda l, j: (l, 0, j)),
            pl.BlockSpec((None, 1, tn), lambda l, j: (l, 0, j)),
        ],
        out_specs=pl.BlockSpec((None, rows, tn), lambda l, j: (l, 0, j)),
        compiler_params=_cparams(2),
        name="ada_mod",
    )(cond, w_mod, b_mod.reshape(depth, 1, n))


def _qkv_kernel(x_ref, mod_ref, g_ref, w_ref, tab_ref, o_ref, u_ref, *, tm, tn, seq, n_batch):
    i = pl.program_id(0)
    d = x_ref.shape[1]

    @pl.when(pl.program_id(1) == 0)
    def _():
        r = jnp.minimum(lax.div(i * tm, seq), n_batch)
        u_ref[...] = _prenorm(x_ref[...], g_ref[...], mod_ref, r, 0, d).astype(BF16)

    rb = _tile(256, tm)
    for b in range(tm // rb):
        rows = slice(b * rb, (b + 1) * rb)
        y = jnp.dot(u_ref[rows, :], w_ref[...], preferred_element_type=F32)
        cos = tab_ref[0, rows, :]
        sa = tab_ref[1, rows, :]
        sb = tab_ref[2, rows, :]
        for c in range(tn // LANES):
            yc = y[:, c * LANES:(c + 1) * LANES]
            rot = (yc * cos + pltpu.roll(yc, LANES - AXIS_ROT // 2, 1) * sa
                   + pltpu.roll(yc, AXIS_ROT // 2, 1) * sb)
            o_ref[c, rows, :] = rot.astype(BF16)


def _qkv_tile(n):
    return _tile(512, n // 3)


def _qkv_proj(xu, mod, g, w, rope, *, layer, seq, n_batch, n_lat_rows):
    rows, d = xu.shape
    n_j, _, tn = w.shape
    n = n_j * tn
    tm = _tile(1024, seq, rows - n_lat_rows)
    assert tn % LANES == 0 and (n // 3) % tn == 0
    lat_tiles_per_seq = seq // tm
    n_lat_tiles = n_lat_rows // tm
    n_q_j = n // 3 // tn

    def tab_index(i, j):
        lat = jnp.where(j < n_q_j, 0, jnp.where(j < 2 * n_q_j, 1, 2))
        ctx = jnp.where(j < n_q_j, 3, 2)
        return (jnp.where(i < n_lat_tiles, lat, ctx), 0, i % lat_tiles_per_seq, 0)

    kern = functools.partial(_qkv_kernel, tm=tm, tn=tn, seq=seq, n_batch=n_batch)
    return pl.pallas_call(
        kern,
        out_shape=jax.ShapeDtypeStruct((n // LANES, rows, LANES), BF16),
        grid=(rows // tm, n // tn),
        in_specs=[
            pl.BlockSpec((tm, d), lambda i, j: (i, 0)),
            pl.BlockSpec((None,) + mod.shape[1:], lambda i, j: (layer, 0, 0)),
            pl.BlockSpec((None, 1, d), lambda i, j: (layer, 0, 0)),
            pl.BlockSpec((None, d, tn), lambda i, j: (j, 0, 0)),
            pl.BlockSpec((None, 3, tm, LANES), tab_index),
        ],
        out_specs=pl.BlockSpec((tn // LANES, tm, LANES), lambda i, j: (j, i, 0)),
        scratch_shapes=[pltpu.VMEM((tm, d), BF16)],
        compiler_params=_cparams(2),
        name="qkv_proj",
    )(xu, mod, g, w, rope)


def _rope_tables(seq):
    rows = seq // GRID_W
    row = jnp.repeat(jnp.arange(rows), GRID_W).astype(F32)
    col = jnp.tile(jnp.arange(GRID_W), rows).astype(F32)
    n_freq = AXIS_ROT // 2
    inv = 1.0 / (ROPE_BASE ** (jnp.arange(n_freq, dtype=F32) / n_freq))
    ang_r = row[:, None] * inv
    ang_c = col[:, None] * inv
    cr, sr, cc, sc = jnp.cos(ang_r), jnp.sin(ang_r), jnp.cos(ang_c), jnp.sin(ang_c)
    z = jnp.zeros_like(sr)
    reps = LANES // HEAD_DIM
    cos = jnp.tile(jnp.concatenate([cr, cr, cc, cc], axis=1), (1, reps))
    sa = jnp.tile(jnp.concatenate([-sr, z, -sc, z], axis=1), (1, reps))
    sb = jnp.tile(jnp.concatenate([z, sr, z, sc], axis=1), (1, reps))
    rot = jnp.stack([cos, sa, sb])
    ident = jnp.stack([jnp.ones_like(cos), jnp.zeros_like(cos), jnp.zeros_like(cos)])
    qs = math.log2(math.e) / math.sqrt(HEAD_DIM)
    return jnp.stack([rot * qs, rot, ident, ident * qs])


def _attn_kernel(*refs, n_seg, lam_init, rb):
    q_ref = refs[0]
    seg_refs = refs[1:1 + 2 * n_seg]
    lq1_ref, lk1_ref, lq2_ref, lk2_ref, gs_ref, o_ref, kk_ref, va_ref = refs[1 + 2 * n_seg:]

    @pl.when(pl.program_id(2) == 0)
    def _():
        off = 0
        for s in range(n_seg):
            n = seg_refs[2 * s].shape[0]
            kk_ref[off:off + n, :] = seg_refs[2 * s][...]
            va_ref[off:off + n, 0:V_HEAD_DIM] = seg_refs[2 * s + 1][...]
            off += n
        va_ref[:, V_HEAD_DIM:] = jnp.ones((va_ref.shape[0], V_HEAD_DIM), BF16)

    lam = _lambda(lq1_ref, lk1_ref, lq2_ref, lk2_ref, lam_init)
    gs = gs_ref[...]
    lane = lax.broadcasted_iota(jnp.int32, (rb, LANES), 1)
    n_blocks = q_ref.shape[0] // rb

    def scores(r):
        q = q_ref[r * rb:(r + 1) * rb, :]
        qq = jnp.concatenate([jnp.where(lane < HEAD_DIM, q, jnp.zeros_like(q)),
                              jnp.where(lane >= HEAD_DIM, q, jnp.zeros_like(q))], axis=0)
        return lax.dot_general(qq, kk_ref[...], (((1,), (1,)), ((), ())),
                               preferred_element_type=F32)

    def finish(r, s):
        p = jnp.exp2(s - jnp.max(s, axis=-1, keepdims=True)).astype(BF16)
        acc = jnp.dot(p, va_ref[...], preferred_element_type=F32)
        ratio = acc[:, :V_HEAD_DIM] / acc[:, V_HEAD_DIM:]
        o = ratio[:rb] - lam * ratio[rb:]
        o = _rms(o, gs, SUBLN_EPS) * (1.0 - lam_init)
        o_ref[r * rb:(r + 1) * rb, :] = o.astype(BF16)

    s_next = scores(0)
    for r in range(n_blocks):
        s_cur = s_next
        if r + 1 < n_blocks:
            s_next = scores(r + 1)
        finish(r, s_cur)


def _attention(qkv, lam_params, gs, *, ia, lam_init, n_batch, q_len, q_row0, segs):
    n_cols = qkv.shape[0]
    n_heads = n_cols // 3
    tq = _tile(2048, q_len)
    n_qt = q_len // tq
    nk = sum(n for _, n in segs)
    q_blk0 = q_row0 // tq
    in_specs = [pl.BlockSpec((None, tq, LANES),
                             lambda b, h, qi: (h, q_blk0 + b * n_qt + qi, 0))]
    args = [qkv]
    for row0, n in segs:
        blk0 = row0 // n
        assert row0 % n == 0
        in_specs.append(pl.BlockSpec(
            (None, n, LANES), lambda b, h, qi, blk0=blk0: (n_heads + h, blk0 + b, 0)))
        in_specs.append(pl.BlockSpec(
            (None, n, LANES), lambda b, h, qi, blk0=blk0: (2 * n_heads + h, blk0 + b, 0)))
        args += [qkv, qkv]
    for p in lam_params:
        in_specs.append(pl.BlockSpec((None, 1, HEAD_DIM), lambda b, h, qi: (ia, 0, 0)))
        args.append(p)
    in_specs.append(pl.BlockSpec((None, 1, V_HEAD_DIM), lambda b, h, qi: (ia, 0, 0)))
    args.append(gs)
    return pl.pallas_call(
        functools.partial(_attn_kernel, n_seg=len(segs), lam_init=lam_init, rb=_tile(128, tq)),
        out_shape=jax.ShapeDtypeStruct((n_batch * q_len, n_heads * V_HEAD_DIM), BF16),
        grid=(n_batch, n_heads, n_qt),
        in_specs=in_specs,
        out_specs=pl.BlockSpec((tq, V_HEAD_DIM), lambda b, h, qi: (b * n_qt + qi, h)),
        scratch_shapes=[pltpu.VMEM((nk, LANES), BF16), pltpu.VMEM((nk, 2 * V_HEAD_DIM), BF16)],
        compiler_params=_cparams(3),
        name="diff_attn",
    )(*args)


def _lambda(lq1_ref, lk1_ref, lq2_ref, lk2_ref, lam_init):
    return (jnp.exp(jnp.sum(lq1_ref[...] * lk1_ref[...], axis=-1, keepdims=True))
            - jnp.exp(jnp.sum(lq2_ref[...] * lk2_ref[...], axis=-1, keepdims=True)) + lam_init)


def _proj_res_kernel(*refs, tm, seq, n_batch, n_lat_tiles, has_ctx):
    if has_ctx:
        inl_ref, inc_ref, x_ref, mod_ref, g_ref, w_ref, o_ref = refs
    else:
        inl_ref, x_ref, mod_ref, g_ref, w_ref, o_ref = refs
    i = pl.program_id(0)
    d = x_ref.shape[1]

    def body(inp):
        y = jnp.dot(inp, w_ref[...], preferred_element_type=F32)
        r = jnp.minimum(lax.div(i * tm, seq), n_batch)
        gate = _mod_row(mod_ref, r, 2, d)
        o_ref[...] = x_ref[...] + gate * _rms(y, g_ref[...], NORM_EPS)

    if has_ctx:
        @pl.when(i < n_lat_tiles)
        def _():
            body(inl_ref[...])

        @pl.when(i >= n_lat_tiles)
        def _():
            body(inc_ref[...])
    else:
        body(inl_ref[...])


def _proj_res(inp_l, inp_c, xu, mod, g, w, *, layer, seq, n_batch):
    d = xu.shape[1]
    n_lat_rows = inp_l.shape[0]
    has_ctx = inp_c is not None
    rows = n_lat_rows + (inp_c.shape[0] if has_ctx else 0)
    tm = _tile(512, seq, *([inp_c.shape[0]] if has_ctx else []))
    n_lat_tiles = n_lat_rows // tm
    kin = inp_l.shape[1]
    in_specs = [pl.BlockSpec((tm, kin), lambda i: (jnp.minimum(i, n_lat_tiles - 1), 0))]
    args = [inp_l]
    if has_ctx:
        in_specs.append(pl.BlockSpec((tm, kin), lambda i: (jnp.maximum(i - n_lat_tiles, 0), 0)))
        args.append(inp_c)
    in_specs += [
        pl.BlockSpec((tm, d), lambda i: (i, 0)),
        pl.BlockSpec((None,) + mod.shape[1:], lambda i: (layer, 0, 0)),
        pl.BlockSpec((None, 1, d), lambda i: (layer, 0, 0)),
        pl.BlockSpec(w.shape, lambda i: (0, 0)),
    ]
    args += [xu, mod, g, w]
    kern = functools.partial(_proj_res_kernel, tm=tm, seq=seq, n_batch=n_batch,
                             n_lat_tiles=n_lat_tiles, has_ctx=has_ctx)
    return pl.pallas_call(
        kern,
        out_shape=jax.ShapeDtypeStruct((rows, d), F32),
        grid=(rows // tm,),
        in_specs=in_specs,
        out_specs=pl.BlockSpec((tm, d), lambda i: (i, 0)),
        compiler_params=_cparams(1),
        name="proj_res",
    )(*args)


def _mlp_kernel(x_ref, mod_ref, gpre_ref, gpost_ref, w1_ref, w2_ref, o_ref, u_ref, h_ref, y_ref,
                *, tm, seq, n_batch):
    i = pl.program_id(0)
    j = pl.program_id(1)
    d = x_ref.shape[1]
    n1, _, tf = h_ref.shape
    n2, _, tn = y_ref.shape
    r = jnp.minimum(lax.div(i * tm, seq), n_batch)

    @pl.when(j == 0)
    def _():
        u_ref[...] = _prenorm(x_ref[...], gpre_ref[...], mod_ref, r, 3, d).astype(BF16)

    @pl.when(j < n1)
    def _():
        h = jnp.maximum(jnp.dot(u_ref[...], w1_ref[...], preferred_element_type=F32), 0.0)
        h_ref[j] = (h * h).astype(BF16)

    @pl.when(j >= n1)
    def _():
        acc = jnp.dot(h_ref[0], w2_ref[0:tf, :], preferred_element_type=F32)
        for k in range(1, n1):
            acc += jnp.dot(h_ref[k], w2_ref[k * tf:(k + 1) * tf, :], preferred_element_type=F32)
        y_ref[j - n1] = acc

    @pl.when(j == n1 + n2 - 1)
    def _():
        ss = jnp.sum(y_ref[0] * y_ref[0], axis=-1, keepdims=True)
        for k in range(1, n2):
            ss += jnp.sum(y_ref[k] * y_ref[k], axis=-1, keepdims=True)
        inv = lax.rsqrt(ss / d + NORM_EPS)
        for k in range(n2):
            cols = slice(k * tn, (k + 1) * tn)
            gate = mod_ref[pl.ds(r, 1), 5 * d + k * tn:5 * d + (k + 1) * tn]
            o_ref[:, cols] = x_ref[:, cols] + gate * (y_ref[k] * inv * gpost_ref[:, cols])


def _col_blocks(w, tn):
    *lead, k, n = w.shape
    w = w.reshape(*lead, k, n // tn, tn)
    return jnp.swapaxes(w, -3, -2)


def _mlp_tiles(d, dff):
    return _tile(1024, dff), _tile(256, d)


def _mlp(xu, mod, gpre, gpost, w1, w2, *, layer, seq, n_batch, rows):
    d = xu.shape[1]
    _, n1, _, tf = w1.shape
    _, n2, dff, tn = w2.shape
    tm = _tile(512, seq, *([xu.shape[0] - n_batch * seq] if rows > n_batch * seq else []))
    kern = functools.partial(_mlp_kernel, tm=tm, seq=seq, n_batch=n_batch)
    return pl.pallas_call(
        kern,
        out_shape=jax.ShapeDtypeStruct((rows, d), F32),
        grid=(rows // tm, n1 + n2),
        in_specs=[
            pl.BlockSpec((tm, d), lambda i, j: (i, 0)),
            pl.BlockSpec((None,) + mod.shape[1:], lambda i, j: (layer, 0, 0)),
            pl.BlockSpec((None, 1, d), lambda i, j: (layer, 0, 0)),
            pl.BlockSpec((None, 1, d), lambda i, j: (layer, 0, 0)),
            pl.BlockSpec((None, None, d, tf), lambda i, j: (layer, jnp.minimum(j, n1 - 1), 0, 0)),
            pl.BlockSpec((None, None, dff, tn),
                         lambda i, j: (layer, jnp.maximum(j - n1, 0), 0, 0)),
        ],
        out_specs=pl.BlockSpec((tm, d), lambda i, j: (i, 0)),
        scratch_shapes=[pltpu.VMEM((tm, d), BF16), pltpu.VMEM((n1, tm, tf), BF16),
                        pltpu.VMEM((n2, tm, tn), F32)],
        compiler_params=_cparams(2),
        name="sq_relu_mlp",
    )(xu, mod, gpre, gpost, w1, w2)


def _dft_tables(n):
    idx = jnp.arange(n, dtype=jnp.int32)
    m = (idx[:, None] * idx[None, :]) % n
    ang = m.astype(F32) * (2.0 * math.pi / n)
    s = 1.0 / math.sqrt(n)
    return jnp.cos(ang) * s, jnp.sin(ang) * s


def _fourier_ch_kernel(x_ref, mod_ref, g_ref, cs_ref, a_ref, b_ref, *, tm, seq, n_batch):
    i = pl.program_id(0)
    d = x_ref.shape[1]
    cg = d // N_FFT_GROUPS
    r = jnp.minimum(lax.div(i * tm, seq), n_batch)
    u = _prenorm(x_ref[...], g_ref[...], mod_ref, r, 0, d).astype(BF16)
    for g in range(N_FFT_GROUPS):
        ab = jnp.dot(u[:, g * cg:(g + 1) * cg], cs_ref[...], preferred_element_type=F32)
        a_ref[:, g * cg:(g + 1) * cg] = ab[:, :cg].astype(BF16)
        b_ref[:, g * cg:(g + 1) * cg] = ab[:, cg:].astype(BF16)


def _fourier_channels(xu, mod, g, cs, *, layer, seq, n_batch, rows):
    d = xu.shape[1]
    tm = _tile(512, seq, *([xu.shape[0] - n_batch * seq] if rows > n_batch * seq else []))
    kern = functools.partial(_fourier_ch_kernel, tm=tm, seq=seq, n_batch=n_batch)
    return pl.pallas_call(
        kern,
        out_shape=[jax.ShapeDtypeStruct((rows, d), BF16)] * 2,
        grid=(rows // tm,),
        in_specs=[
            pl.BlockSpec((tm, d), lambda i: (i, 0)),
            pl.BlockSpec((None,) + mod.shape[1:], lambda i: (layer, 0, 0)),
            pl.BlockSpec((None, 1, d), lambda i: (layer, 0, 0)),
            pl.BlockSpec(cs.shape, lambda i: (0, 0)),
        ],
        out_specs=[pl.BlockSpec((tm, d), lambda i: (i, 0))] * 2,
        compiler_params=_cparams(1),
        name="fourier_channels",
    )(xu, mod, g, cs)


def _fourier_seq_kernel(tc_ref, tms_ref, a_ref, b_ref, o_ref, *, tmm):
    n = tc_ref.shape[0]
    for mi in range(n // tmm):
        rows = slice(mi * tmm, (mi + 1) * tmm)
        y = (jnp.dot(tc_ref[rows, :], a_ref[...], preferred_element_type=F32)
             + jnp.dot(tms_ref[rows, :], b_ref[...], preferred_element_type=F32))
        o_ref[rows, :] = y.astype(BF16)


def _fourier_seq(a, b, tc, tms, *, n_batch, length, row0):
    d = a.shape[1]
    tn = _tile(512, d)
    tmm = _tile(512, length)
    blk0 = row0 // length
    assert row0 % length == 0
    ab_spec = pl.BlockSpec((length, tn), lambda bi, ni: (blk0 + bi, ni))
    tab_spec = pl.BlockSpec((length, length), lambda bi, ni: (0, 0))
    return pl.pallas_call(
        functools.partial(_fourier_seq_kernel, tmm=tmm),
        out_shape=jax.ShapeDtypeStruct((n_batch * length, d), BF16),
        grid=(n_batch, d // tn),
        in_specs=[tab_spec, tab_spec, ab_spec, ab_spec],
        out_specs=pl.BlockSpec((length, tn), lambda bi, ni: (bi, ni)),
        compiler_params=_cparams(2),
        name="fourier_seq",
    )(tc, tms, a, b)


def _pool_kernel(x_ref, xp_ref, xn_ref, mod_ref, gpre_ref, gpost_ref, wp_ref, ps_ref, o_ref,
                 ext_ref, y_ref, *, tm, seq, ctx_len, n_batch, n_lat_rows):
    i = pl.program_id(0)
    d = x_ref.shape[1]
    cg = d // len(POOL_WINDOWS)
    row0 = i * tm
    is_lat = row0 < n_lat_rows
    slen = jnp.where(is_lat, seq, ctx_len)
    pos0 = jnp.where(is_lat, lax.rem(row0, seq), lax.rem(row0 - n_lat_rows, ctx_len))
    r = jnp.minimum(lax.div(row0, seq), n_batch)
    g_pre = gpre_ref[...]
    keep_prev = jnp.where(pos0 > 0, 1.0, 0.0).astype(F32)
    keep_next = jnp.where(pos0 + tm < slen, 1.0, 0.0).astype(F32)
    ext_ref[0:POOL_HALO, :] = _prenorm(xp_ref[...], g_pre, mod_ref, r, 0, d) * keep_prev
    ext_ref[POOL_HALO:POOL_HALO + tm, :] = _prenorm(x_ref[...], g_pre, mod_ref, r, 0, d)
    ext_ref[POOL_HALO + tm:, :] = _prenorm(xn_ref[...], g_pre, mod_ref, r, 0, d) * keep_next

    pos = pos0 + lax.broadcasted_iota(jnp.int32, (tm, 1), 0)
    for g, w in enumerate(POOL_WINDOWS):
        cols = slice(g * cg, (g + 1) * cg)
        acc = ext_ref[POOL_HALO - w // 2:POOL_HALO - w // 2 + tm, cols]
        for dlt in range(-w // 2 + 1, w // 2):
            acc = acc + ext_ref[POOL_HALO + dlt:POOL_HALO + dlt + tm, cols]
        lo = jnp.maximum(pos - w // 2, 0)
        hi = jnp.minimum(pos - w // 2 + w, slen)
        mean = acc / (hi - lo).astype(F32)
        diff = (mean - ext_ref[POOL_HALO:POOL_HALO + tm, cols]).astype(BF16)
        y_ref[:, cols] = jnp.dot(diff, wp_ref[g], preferred_element_type=F32) * ps_ref[:, cols]

    gate = _mod_row(mod_ref, r, 2, d)
    o_ref[...] = x_ref[...] + gate * _rms(y_ref[...], gpost_ref[...], NORM_EPS)


def _pool_mix(xu, mod, gpre, gpost, wp, ps, *, layer, ic, seq, ctx_len, n_batch, rows):
    d = xu.shape[1]
    n_lat_rows = n_batch * seq
    tm = _tile(256, seq, *([ctx_len] if rows > n_lat_rows else []))
    assert tm % POOL_HALO == 0 and POOL_HALO >= max(POOL_WINDOWS) // 2
    hb = tm // POOL_HALO
    last_hblk = xu.shape[0] // POOL_HALO - 1
    kern = functools.partial(_pool_kernel, tm=tm, seq=seq, ctx_len=ctx_len, n_batch=n_batch,
                             n_lat_rows=n_lat_rows)
    return pl.pallas_call(
        kern,
        out_shape=jax.ShapeDtypeStruct((rows, d), F32),
        grid=(rows // tm,),
        in_specs=[
            pl.BlockSpec((tm, d), lambda i: (i, 0)),
            pl.BlockSpec((POOL_HALO, d), lambda i: (jnp.maximum(i * hb - 1, 0), 0)),
            pl.BlockSpec((POOL_HALO, d), lambda i: (jnp.minimum((i + 1) * hb, last_hblk), 0)),
            pl.BlockSpec((None,) + mod.shape[1:], lambda i: (layer, 0, 0)),
            pl.BlockSpec((None, 1, d), lambda i: (layer, 0, 0)),
            pl.BlockSpec((None, 1, d), lambda i: (layer, 0, 0)),
            pl.BlockSpec((None,) + wp.shape[1:], lambda i: (ic, 0, 0, 0)),
            pl.BlockSpec((None, 1, d), lambda i: (ic, 0, 0)),
        ],
        out_specs=pl.BlockSpec((tm, d), lambda i: (i, 0)),
        scratch_shapes=[pltpu.VMEM((tm + 2 * POOL_HALO, d), F32), pltpu.VMEM((tm, d), F32)],
        compiler_params=_cparams(1),
        name="pool_mix",
    )(xu, xu, xu, mod, gpre, gpost, wp, ps)


def kernel(x, c, ctx, c_ctx, w_mod, b_mod, g_mix_pre, g_mix_post, g_mlp_pre, g_mlp_post,
           w_qkv, w_attn_out, lambda_q1, lambda_k1, lambda_q2, lambda_k2, g_subln,
           w_fourier_out, w_pool, pool_scale, w_mlp_in, w_mlp_out):
    n_batch, seq, d = x.shape
    ctx_len = ctx.shape[1]
    depth = w_mod.shape[0]
    n_lat_rows = n_batch * seq
    n_ctx_rows = n_batch * ctx_len
    n_rows = n_lat_rows + n_ctx_rows

    xu = jnp.concatenate([x.reshape(n_lat_rows, d), ctx.reshape(n_ctx_rows, d)], axis=0)
    n_cond = -(-(n_batch + 1) // 8) * 8
    cond = jnp.concatenate(
        [c, c_ctx[None, :], jnp.zeros((n_cond - n_batch - 1, d), F32)], axis=0)
    mod = _mod_all(cond, w_mod, b_mod)

    w_qkv_b = _col_blocks(w_qkv.astype(BF16), _qkv_tile(w_qkv.shape[2]))
    w_o_b = w_attn_out.astype(BF16)
    w_f_b = w_fourier_out.astype(BF16)
    w_p_b = w_pool.astype(BF16)
    tf, tn = _mlp_tiles(d, w_mlp_in.shape[2])
    w1_b = _col_blocks(w_mlp_in.astype(BF16), tf)
    w2_b = _col_blocks(w_mlp_out.astype(BF16), tn)

    row3 = lambda t: t.reshape(t.shape[0], 1, t.shape[1])
    g_mix_pre, g_mix_post, g_mlp_pre, g_mlp_post = map(
        row3, (g_mix_pre, g_mix_post, g_mlp_pre, g_mlp_post))
    lambda_q1, lambda_k1, lambda_q2, lambda_k2, g_subln, pool_scale = map(
        row3, (lambda_q1, lambda_k1, lambda_q2, lambda_k2, g_subln, pool_scale))

    rope = _rope_tables(seq)
    common = dict(seq=seq, n_batch=n_batch)

    ia = ib = ic = 0
    for i in range(depth):
        last = i == depth - 1
        kind = i % N_MIXERS
        out_rows = n_lat_rows if last else n_rows
        if kind == 0:
            lam_init = 0.8 - 0.6 * math.exp(-0.3 * i)
            qkv = _qkv_proj(xu, mod, g_mix_pre, w_qkv_b[ia], rope, layer=i,
                            n_lat_rows=n_lat_rows, **common)
            lam_params = (lambda_q1, lambda_k1, lambda_q2, lambda_k2)
            akw = dict(ia=ia, lam_init=lam_init, n_batch=n_batch)
            o_l = _attention(qkv, lam_params, g_subln, q_len=seq, q_row0=0,
                             segs=[(0, seq), (n_lat_rows, ctx_len)], **akw)
            o_c = None if last else _attention(
                qkv, lam_params, g_subln, q_len=ctx_len, q_row0=n_lat_rows,
                segs=[(n_lat_rows, ctx_len)], **akw)
            xu = _proj_res(o_l, o_c, xu, mod, g_mix_post, w_o_b[ia], layer=i, **common)
            ia += 1
        elif kind == 1:
            cg = d // N_FFT_GROUPS
            cc, sc = _dft_tables(cg)
            cs = jnp.concatenate([cc, sc], axis=1).astype(BF16)
            a, b = _fourier_channels(xu, mod, g_mix_pre, cs, layer=i, rows=out_rows, **common)
            tc, ts = _dft_tables(seq)
            y_l = _fourier_seq(a, b, tc.astype(BF16), (-ts).astype(BF16),
                               n_batch=n_batch, length=seq, row0=0)
            y_c = None
            if not last:
                tcc, tsc = _dft_tables(ctx_len)
                y_c = _fourier_seq(a, b, tcc.astype(BF16), (-tsc).astype(BF16),
                                   n_batch=n_batch, length=ctx_len, row0=n_lat_rows)
            xu = _proj_res(y_l, y_c, xu, mod, g_mix_post, w_f_b[ib], layer=i, **common)
            ib += 1
        else:
            xu = _pool_mix(xu, mod, g_mix_pre, g_mix_post, w_p_b, pool_scale, layer=i, ic=ic,
                           ctx_len=ctx_len, rows=out_rows, **common)
            ic += 1
        xu = _mlp(xu, mod, g_mlp_pre, g_mlp_post, w1_b, w2_b, layer=i, rows=out_rows, **common)
    return xu[:n_lat_rows].reshape(n_batch, seq, d)
```

```python
import functools
import math

import jax
import jax.numpy as jnp
from jax import lax
from jax.experimental import pallas as pl
from jax.experimental.pallas import tpu as pltpu

F32 = jnp.float32
BF16 = jnp.bfloat16

GRID_W = 64
N_MIXERS = 3
HEAD_DIM = 64
V_HEAD_DIM = 2 * HEAD_DIM
ROPE_BASE = 10000.0
AXIS_ROT = HEAD_DIM // 2
SUBLN_EPS = 1e-5
N_FFT_GROUPS = 4
POOL_WINDOWS = (2, 4, 8, 16)
NORM_EPS = 1e-6
N_MOD = 6
POOL_HALO = 8

LANES = 128
V7X_VMEM_BYTES = 64 * 1024 * 1024
VMEM_LIMIT = V7X_VMEM_BYTES - 8 * 1024 * 1024


def _cparams(n_axes):
    return pltpu.CompilerParams(
        dimension_semantics=("arbitrary",) * n_axes, vmem_limit_bytes=VMEM_LIMIT)


def _tile(pref, *counts):
    t = min(pref, *counts)
    while t > 8 and any(c % t for c in counts):
        t -= 8
    assert t >= 8 and all(c % t == 0 for c in counts), (pref, counts)
    return t


def _rms(x, g, eps):
    return x * lax.rsqrt(jnp.mean(x * x, axis=-1, keepdims=True) + eps) * g


def _mod_row(mod_ref, r, k, d):
    return mod_ref[pl.ds(r, 1), k * d:(k + 1) * d]


def _prenorm(x, g, mod_ref, r, k_shift, d):
    sh = _mod_row(mod_ref, r, k_shift, d)
    sc = _mod_row(mod_ref, r, k_shift + 1, d)
    return _rms(x, g, NORM_EPS) * (1.0 + sc) + sh


def _mod_spec(mod, layer, n_axes):
    zeros = (0,) * 2
    return pl.BlockSpec((None,) + mod.shape[1:], lambda *_: (layer,) + zeros)


def _vec_spec(width, index):
    return pl.BlockSpec((None, 1, width), lambda *_: (index, 0, 0))


def _mod_kernel(cond_ref, w_ref, b_ref, o_ref):
    s = jax.nn.silu(cond_ref[...]).astype(BF16)
    o_ref[...] = jnp.dot(s, w_ref[...].astype(BF16), preferred_element_type=F32) + b_ref[...]


def _mod_all(cond, w_mod, b_mod):
    depth, d, n = w_mod.shape
    rows = cond.shape[0]
    tn = _tile(1024, n)
    return pl.pallas_call(
        _mod_kernel,
        out_shape=jax.ShapeDtypeStruct((depth, rows, n), F32),
        grid=(depth, n // tn),
        in_specs=[
            pl.BlockSpec((rows, d), lambda l, j: (0, 0)),
            pl.BlockSpec((None, d, tn), lambda l, j: (l, 0, j)),
            pl.BlockSpec((None, 1, tn), lambda l, j: (l, 0, j)),
        ],
        out_specs=pl.BlockSpec((None, rows, tn), lambda l, j: (l, 0, j)),
        compiler_params=_cparams(2),
        name="ada_mod",
    )(cond, w_mod, b_mod.reshape(depth, 1, n))


def _qkv_kernel(x_ref, mod_ref, g_ref, w_ref, tab_ref, o_ref, u_ref, *, tm, tn, seq, n_batch, row0):
    i = pl.program_id(0)
    d = x_ref.shape[1]

    @pl.when(pl.program_id(1) == 0)
    def _():
        r = jnp.minimum(lax.div(row0 + i * tm, seq), n_batch)
        u_ref[...] = _prenorm(x_ref[...], g_ref[...], mod_ref, r, 0, d).astype(BF16)

    rb = _tile(256, tm)
    for b in range(tm // rb):
        rows = slice(b * rb, (b + 1) * rb)
        y = jnp.dot(u_ref[rows, :], w_ref[...], preferred_element_type=F32)
        cos = tab_ref[0, rows, :]
        sa = tab_ref[1, rows, :]
        sb = tab_ref[2, rows, :]
        for c in range(tn // LANES):
            yc = y[:, c * LANES:(c + 1) * LANES]
            rot = (yc * cos + pltpu.roll(yc, LANES - AXIS_ROT // 2, 1) * sa
                   + pltpu.roll(yc, AXIS_ROT // 2, 1) * sb)
            o_ref[c, rows, :] = rot.astype(BF16)


def _qkv_proj(xa, mod, g, w, rope, *, layer, ia, seq, n_batch, row0):
    rows, d = xa.shape
    n = w.shape[2]
    n_lat_rows = n_batch * seq
    n_lat_here = max(0, min(rows, n_lat_rows - row0))
    tm = _tile(1024, seq, *[c for c in (n_lat_here, rows - n_lat_here) if c])
    tn = _tile(512, n // 3)
    assert tn % LANES == 0 and row0 % tm == 0
    lat_tiles_per_seq = seq // tm
    n_lat_tiles = n_lat_here // tm
    n_q_j = n // 3 // tn

    def tab_index(i, j):
        lat = jnp.where(j < n_q_j, 0, jnp.where(j < 2 * n_q_j, 1, 2))
        ctx = jnp.where(j < n_q_j, 3, 2)
        return (jnp.where(i < n_lat_tiles, lat, ctx), 0, (row0 // tm + i) % lat_tiles_per_seq, 0)

    kern = functools.partial(_qkv_kernel, tm=tm, tn=tn, seq=seq, n_batch=n_batch, row0=row0)
    return pl.pallas_call(
        kern,
        out_shape=jax.ShapeDtypeStruct((n // LANES, rows, LANES), BF16),
        grid=(rows // tm, n // tn),
        in_specs=[
            pl.BlockSpec((tm, d), lambda i, j: (i, 0)),
            _mod_spec(mod, layer, 2),
            _vec_spec(d, layer),
            pl.BlockSpec((None, d, tn), lambda i, j: (ia, 0, j)),
            pl.BlockSpec((None, 3, tm, LANES), tab_index),
        ],
        out_specs=pl.BlockSpec((tn // LANES, tm, LANES), lambda i, j: (j, i, 0)),
        scratch_shapes=[pltpu.VMEM((tm, d), BF16)],
        compiler_params=_cparams(2),
        name="qkv_proj",
    )(xa, mod, g, w, rope)


def _rope_tables(seq):
    rows = seq // GRID_W
    row = jnp.repeat(jnp.arange(rows), GRID_W).astype(F32)
    col = jnp.tile(jnp.arange(GRID_W), rows).astype(F32)
    n_freq = AXIS_ROT // 2
    inv = 1.0 / (ROPE_BASE ** (jnp.arange(n_freq, dtype=F32) / n_freq))
    ang_r = row[:, None] * inv
    ang_c = col[:, None] * inv
    cr, sr, cc, sc = jnp.cos(ang_r), jnp.sin(ang_r), jnp.cos(ang_c), jnp.sin(ang_c)
    z = jnp.zeros_like(sr)
    reps = LANES // HEAD_DIM
    cos = jnp.tile(jnp.concatenate([cr, cr, cc, cc], axis=1), (1, reps))
    sa = jnp.tile(jnp.concatenate([-sr, z, -sc, z], axis=1), (1, reps))
    sb = jnp.tile(jnp.concatenate([z, sr, z, sc], axis=1), (1, reps))
    rot = jnp.stack([cos, sa, sb])
    ident = jnp.stack([jnp.ones_like(cos), jnp.zeros_like(cos), jnp.zeros_like(cos)])
    qs = math.log2(math.e) / math.sqrt(HEAD_DIM)
    return jnp.stack([rot * qs, rot, ident, ident * qs])


def _lambda(lq1_ref, lk1_ref, lq2_ref, lk2_ref, lam_init):
    return (jnp.exp(jnp.sum(lq1_ref[...] * lk1_ref[...], axis=-1, keepdims=True))
            - jnp.exp(jnp.sum(lq2_ref[...] * lk2_ref[...], axis=-1, keepdims=True)) + lam_init)


def _attn_kernel(*refs, n_seg, lam_init, rb):
    q_ref = refs[0]
    seg_refs = refs[1:1 + 2 * n_seg]
    lq1_ref, lk1_ref, lq2_ref, lk2_ref, gs_ref, o_ref, kk_ref, va_ref = refs[1 + 2 * n_seg:]
    hb, q_len, _ = q_ref.shape
    n_blocks = q_len // rb
    lam = _lambda(lq1_ref, lk1_ref, lq2_ref, lk2_ref, lam_init)
    gs = gs_ref[...]
    lane = lax.broadcasted_iota(jnp.int32, (rb, LANES), 1)
    va_ref[:, V_HEAD_DIM:] = jnp.ones((va_ref.shape[0], V_HEAD_DIM), BF16)

    for h in range(hb):
        off = 0
        for s in range(n_seg):
            n = seg_refs[2 * s].shape[1]
            kk_ref[off:off + n, :] = seg_refs[2 * s][h]
            va_ref[off:off + n, 0:V_HEAD_DIM] = seg_refs[2 * s + 1][h]
            off += n

        def scores(r):
            q = q_ref[h, r * rb:(r + 1) * rb, :]
            qq = jnp.concatenate([jnp.where(lane < HEAD_DIM, q, jnp.zeros_like(q)),
                                  jnp.where(lane >= HEAD_DIM, q, jnp.zeros_like(q))], axis=0)
            return lax.dot_general(qq, kk_ref[...], (((1,), (1,)), ((), ())),
                                   preferred_element_type=F32)

        def finish(r, s):
            p = jnp.exp2(s - jnp.max(s, axis=-1, keepdims=True)).astype(BF16)
            acc = jnp.dot(p, va_ref[...], preferred_element_type=F32)
            ratio = acc[:, :V_HEAD_DIM] / acc[:, V_HEAD_DIM:]
            o = ratio[:rb] - lam * ratio[rb:]
            o = _rms(o, gs, SUBLN_EPS) * (1.0 - lam_init)
            o_ref[r * rb:(r + 1) * rb, h * V_HEAD_DIM:(h + 1) * V_HEAD_DIM] = o.astype(BF16)

        s_next = scores(0)
        for r in range(n_blocks):
            s_cur = s_next
            if r + 1 < n_blocks:
                s_next = scores(r + 1)
            finish(r, s_cur)


def _attention(q_arr, q_row0, q_len, segs, lam_params, gs, *, ia, lam_init, n_batch, hb):
    n_heads = q_arr.shape[0] // 3
    assert n_heads % hb == 0 and q_row0 % q_len == 0
    nk = sum(n for _, _, n in segs)
    q_blk0 = q_row0 // q_len
    in_specs = [pl.BlockSpec((hb, q_len, LANES), lambda b, h: (h, q_blk0 + b, 0))]
    args = [q_arr]
    for arr, row0, n in segs:
        assert row0 % n == 0 and arr.shape[0] == 3 * n_heads
        for part in (1, 2):
            in_specs.append(pl.BlockSpec(
                (hb, n, LANES),
                lambda b, h, blk0=row0 // n, col0=part * n_heads // hb: (col0 + h, blk0 + b, 0)))
            args.append(arr)
    in_specs += [_vec_spec(HEAD_DIM, ia)] * 4
    in_specs.append(_vec_spec(V_HEAD_DIM, ia))
    return pl.pallas_call(
        functools.partial(_attn_kernel, n_seg=len(segs), lam_init=lam_init,
                          rb=_tile(128, q_len)),
        out_shape=jax.ShapeDtypeStruct((n_batch * q_len, n_heads * V_HEAD_DIM), BF16),
        grid=(n_batch, n_heads // hb),
        in_specs=in_specs,
        out_specs=pl.BlockSpec((q_len, hb * V_HEAD_DIM), lambda b, h: (b, h)),
        scratch_shapes=[pltpu.VMEM((nk, LANES), BF16), pltpu.VMEM((nk, 2 * V_HEAD_DIM), BF16)],
        compiler_params=_cparams(2),
        name="diff_attn",
    )(*args, *lam_params, gs)


def _proj_res_kernel(*refs, tm, seq, n_batch, n_lat_tiles, has_ctx, split_x):
    refs = list(refs)
    inl_ref = refs.pop(0)
    inc_ref = refs.pop(0) if has_ctx else None
    xl_ref = refs.pop(0)
    xc_ref = refs.pop(0) if split_x else xl_ref
    mod_ref, g_ref, w_ref, o_ref = refs
    i = pl.program_id(0)
    d = o_ref.shape[1]

    def body(inp_ref, x_ref):
        y = jnp.dot(inp_ref[...], w_ref[...], preferred_element_type=F32)
        r = jnp.minimum(lax.div(i * tm, seq), n_batch)
        gate = _mod_row(mod_ref, r, 2, d)
        o_ref[...] = x_ref[...] + gate * _rms(y, g_ref[...], NORM_EPS)

    if has_ctx:
        @pl.when(i < n_lat_tiles)
        def _():
            body(inl_ref, xl_ref)

        @pl.when(i >= n_lat_tiles)
        def _():
            body(inc_ref, xc_ref)
    else:
        body(inl_ref, xl_ref)


def _proj_res(inp_l, inp_c, x_l, x_c, mod, g, w, *, layer, iw, seq, n_batch):
    d = x_l.shape[1]
    n_lat_rows = inp_l.shape[0]
    has_ctx = inp_c is not None
    split_x = x_c is not None
    assert has_ctx or not split_x
    rows = n_lat_rows + (inp_c.shape[0] if has_ctx else 0)
    tm = _tile(512, seq, *([inp_c.shape[0]] if has_ctx else []))
    n_lat_tiles = n_lat_rows // tm
    kin = inp_l.shape[1]
    lat_index = lambda i: (jnp.minimum(i, n_lat_tiles - 1), 0)
    ctx_index = lambda i: (jnp.maximum(i - n_lat_tiles, 0), 0)
    in_specs = [pl.BlockSpec((tm, kin), lat_index)]
    args = [inp_l]
    if has_ctx:
        in_specs.append(pl.BlockSpec((tm, kin), ctx_index))
        args.append(inp_c)
    if split_x:
        in_specs += [pl.BlockSpec((tm, d), lat_index), pl.BlockSpec((tm, d), ctx_index)]
        args += [x_l, x_c]
    else:
        in_specs.append(pl.BlockSpec((tm, d), lambda i: (i, 0)))
        args.append(x_l)
    in_specs += [
        _mod_spec(mod, layer, 1),
        _vec_spec(d, layer),
        pl.BlockSpec((None,) + w.shape[1:], lambda i: (iw, 0, 0)),
    ]
    args += [mod, g, w]
    kern = functools.partial(_proj_res_kernel, tm=tm, seq=seq, n_batch=n_batch,
                             n_lat_tiles=n_lat_tiles, has_ctx=has_ctx, split_x=split_x)
    return pl.pallas_call(
        kern,
        out_shape=jax.ShapeDtypeStruct((rows, d), F32),
        grid=(rows // tm,),
        in_specs=in_specs,
        out_specs=pl.BlockSpec((tm, d), lambda i: (i, 0)),
        compiler_params=_cparams(1),
        name="proj_res",
    )(*args)


def _mlp_kernel(x_ref, mod_ref, gpre_ref, gpost_ref, w1_ref, w2_ref, o_ref, u_ref, h_ref, y_ref,
                *, tm, seq, n_batch):
    i = pl.program_id(0)
    j = pl.program_id(1)
    d = x_ref.shape[1]
    n1, _, tf = h_ref.shape
    n2, _, tn = y_ref.shape
    r = jnp.minimum(lax.div(i * tm, seq), n_batch)

    @pl.when(j == 0)
    def _():
        u_ref[...] = _prenorm(x_ref[...], gpre_ref[...], mod_ref, r, 3, d).astype(BF16)

    @pl.when(j < n1)
    def _():
        h = jnp.maximum(jnp.dot(u_ref[...], w1_ref[...], preferred_element_type=F32), 0.0)
        h_ref[j] = (h * h).astype(BF16)

    @pl.when(j >= n1)
    def _():
        acc = jnp.dot(h_ref[0], w2_ref[0:tf, :], preferred_element_type=F32)
        for k in range(1, n1):
            acc += jnp.dot(h_ref[k], w2_ref[k * tf:(k + 1) * tf, :], preferred_element_type=F32)
        y_ref[j - n1] = acc

    @pl.when(j == n1 + n2 - 1)
    def _():
        ss = jnp.sum(y_ref[0] * y_ref[0], axis=-1, keepdims=True)
        for k in range(1, n2):
            ss += jnp.sum(y_ref[k] * y_ref[k], axis=-1, keepdims=True)
        inv = lax.rsqrt(ss / d + NORM_EPS)
        for k in range(n2):
            cols = slice(k * tn, (k + 1) * tn)
            gate = mod_ref[pl.ds(r, 1), 5 * d + k * tn:5 * d + (k + 1) * tn]
            o_ref[:, cols] = x_ref[:, cols] + gate * (y_ref[k] * inv * gpost_ref[:, cols])


def _mlp(xu, mod, gpre, gpost, w1, w2, *, layer, seq, n_batch, rows):
    d = xu.shape[1]
    dff = w1.shape[2]
    tm = _tile(512, seq, *([xu.shape[0] - n_batch * seq] if rows > n_batch * seq else []))
    tf = _tile(1024, dff)
    tn = _tile(256, d)
    n1 = dff // tf
    n2 = d // tn
    kern = functools.partial(_mlp_kernel, tm=tm, seq=seq, n_batch=n_batch)
    return pl.pallas_call(
        kern,
        out_shape=jax.ShapeDtypeStruct((rows, d), F32),
        grid=(rows // tm, n1 + n2),
        in_specs=[
            pl.BlockSpec((tm, d), lambda i, j: (i, 0)),
            _mod_spec(mod, layer, 2),
            _vec_spec(d, layer),
            _vec_spec(d, layer),
            pl.BlockSpec((None, d, tf), lambda i, j: (layer, 0, jnp.minimum(j, n1 - 1))),
            pl.BlockSpec((None, dff, tn), lambda i, j: (layer, 0, jnp.maximum(j - n1, 0))),
        ],
        out_specs=pl.BlockSpec((tm, d), lambda i, j: (i, 0)),
        scratch_shapes=[pltpu.VMEM((tm, d), BF16), pltpu.VMEM((n1, tm, tf), BF16),
                        pltpu.VMEM((n2, tm, tn), F32)],
        compiler_params=_cparams(2),
        name="sq_relu_mlp",
    )(xu, mod, gpre, gpost, w1, w2)


def _dft_tables(n):
    b = max(f for f in range(1, math.isqrt(n) + 1) if n % f == 0)
    a = n // b
    j = jnp.arange(n, dtype=jnp.int32)[:, None]
    ang1 = ((j * jnp.arange(a, dtype=jnp.int32)[None, :]) % a).astype(F32) * (2.0 * math.pi / a)
    ang0 = ((j * jnp.arange(b, dtype=jnp.int32)[None, :]) % n).astype(F32) * (2.0 * math.pi / n)
    c1, s1 = jnp.cos(ang1)[:, :, None], jnp.sin(ang1)[:, :, None]
    c0, s0 = jnp.cos(ang0)[:, None, :], jnp.sin(ang0)[:, None, :]
    scale = 1.0 / math.sqrt(n)
    cos = (c1 * c0 - s1 * s0).reshape(n, n) * scale
    sin = (s1 * c0 + c1 * s0).reshape(n, n) * scale
    return cos, sin


def _fourier_ch_kernel(x_ref, mod_ref, g_ref, cs_ref, a_ref, b_ref, *, tm, seq, n_batch):
    i = pl.program_id(0)
    d = x_ref.shape[1]
    cg = d // N_FFT_GROUPS
    r = jnp.minimum(lax.div(i * tm, seq), n_batch)
    u = _prenorm(x_ref[...], g_ref[...], mod_ref, r, 0, d).astype(BF16)
    for g in range(N_FFT_GROUPS):
        ab = jnp.dot(u[:, g * cg:(g + 1) * cg], cs_ref[...], preferred_element_type=F32)
        a_ref[:, g * cg:(g + 1) * cg] = ab[:, :cg].astype(BF16)
        b_ref[:, g * cg:(g + 1) * cg] = ab[:, cg:].astype(BF16)


def _fourier_channels(xu, mod, g, cs, *, layer, seq, n_batch, rows):
    d = xu.shape[1]
    tm = _tile(512, seq, *([xu.shape[0] - n_batch * seq] if rows > n_batch * seq else []))
    kern = functools.partial(_fourier_ch_kernel, tm=tm, seq=seq, n_batch=n_batch)
    return pl.pallas_call(
        kern,
        out_shape=[jax.ShapeDtypeStruct((rows, d), BF16)] * 2,
        grid=(rows // tm,),
        in_specs=[
            pl.BlockSpec((tm, d), lambda i: (i, 0)),
            _mod_spec(mod, layer, 1),
            _vec_spec(d, layer),
            pl.BlockSpec(cs.shape, lambda i: (0, 0)),
        ],
        out_specs=[pl.BlockSpec((tm, d), lambda i: (i, 0))] * 2,
        compiler_params=_cparams(1),
        name="fourier_channels",
    )(xu, mod, g, cs)


def _fourier_seq_kernel(tc_ref, tms_ref, a_ref, b_ref, o_ref, *, tmm):
    n = tc_ref.shape[0]
    for mi in range(n // tmm):
        rows = slice(mi * tmm, (mi + 1) * tmm)
        y = (jnp.dot(tc_ref[rows, :], a_ref[...], preferred_element_type=F32)
             + jnp.dot(tms_ref[rows, :], b_ref[...], preferred_element_type=F32))
        o_ref[rows, :] = y.astype(BF16)


def _fourier_seq(a, b, tc, tms, *, n_batch, length, row0):
    d = a.shape[1]
    tn = _tile(512, d)
    tmm = _tile(512, length)
    blk0 = row0 // length
    assert row0 % length == 0
    ab_spec = pl.BlockSpec((length, tn), lambda bi, ni: (blk0 + bi, ni))
    tab_spec = pl.BlockSpec((length, length), lambda bi, ni: (0, 0))
    return pl.pallas_call(
        functools.partial(_fourier_seq_kernel, tmm=tmm),
        out_shape=jax.ShapeDtypeStruct((n_batch * length, d), BF16),
        grid=(n_batch, d // tn),
        in_specs=[tab_spec, tab_spec, ab_spec, ab_spec],
        out_specs=pl.BlockSpec((length, tn), lambda bi, ni: (bi, ni)),
        compiler_params=_cparams(2),
        name="fourier_seq",
    )(tc, tms, a, b)


def _pool_kernel(x_ref, xp_ref, xn_ref, mod_ref, gpre_ref, gpost_ref, wp_ref, ps_ref, o_ref,
                 ext_ref, y_ref, *, tm, seq, ctx_len, n_batch, n_lat_rows):
    i = pl.program_id(0)
    d = x_ref.shape[1]
    cg = d // len(POOL_WINDOWS)
    row0 = i * tm
    is_lat = row0 < n_lat_rows
    slen = jnp.where(is_lat, seq, ctx_len)
    pos0 = jnp.where(is_lat, lax.rem(row0, seq), lax.rem(row0 - n_lat_rows, ctx_len))
    r = jnp.minimum(lax.div(row0, seq), n_batch)
    g_pre = gpre_ref[...]
    keep_prev = jnp.where(pos0 > 0, 1.0, 0.0).astype(F32)
    keep_next = jnp.where(pos0 + tm < slen, 1.0, 0.0).astype(F32)
    ext_ref[0:POOL_HALO, :] = _prenorm(xp_ref[...], g_pre, mod_ref, r, 0, d) * keep_prev
    ext_ref[POOL_HALO:POOL_HALO + tm, :] = _prenorm(x_ref[...], g_pre, mod_ref, r, 0, d)
    ext_ref[POOL_HALO + tm:, :] = _prenorm(xn_ref[...], g_pre, mod_ref, r, 0, d) * keep_next

    pos = pos0 + lax.broadcasted_iota(jnp.int32, (tm, 1), 0)
    for g, w in enumerate(POOL_WINDOWS):
        cols = slice(g * cg, (g + 1) * cg)
        acc = ext_ref[POOL_HALO - w // 2:POOL_HALO - w // 2 + tm, cols]
        for dlt in range(-w // 2 + 1, w // 2):
            acc = acc + ext_ref[POOL_HALO + dlt:POOL_HALO + dlt + tm, cols]
        lo = jnp.maximum(pos - w // 2, 0)
        hi = jnp.minimum(pos - w // 2 + w, slen)
        mean = acc / (hi - lo).astype(F32)
        diff = (mean - ext_ref[POOL_HALO:POOL_HALO + tm, cols]).astype(BF16)
        y_ref[:, cols] = jnp.dot(diff, wp_ref[g], preferred_element_type=F32) * ps_ref[:, cols]

    gate = _mod_row(mod_ref, r, 2, d)
    o_ref[...] = x_ref[...] + gate * _rms(y_ref[...], gpost_ref[...], NORM_EPS)


def _pool_mix(xu, mod, gpre, gpost, wp, ps, *, layer, ic, seq, ctx_len, n_batch, rows):
    d = xu.shape[1]
    n_lat_rows = n_batch * seq
    tm = _tile(256, seq, *([ctx_len] if rows > n_lat_rows else []))
    assert tm % POOL_HALO == 0 and POOL_HALO >= max(POOL_WINDOWS) // 2
    hb = tm // POOL_HALO
    last_hblk = xu.shape[0] // POOL_HALO - 1
    kern = functools.partial(_pool_kernel, tm=tm, seq=seq, ctx_len=ctx_len, n_batch=n_batch,
                             n_lat_rows=n_lat_rows)
    return pl.pallas_call(
        kern,
        out_shape=jax.ShapeDtypeStruct((rows, d), F32),
        grid=(rows // tm,),
        in_specs=[
            pl.BlockSpec((tm, d), lambda i: (i, 0)),
            pl.BlockSpec((POOL_HALO, d), lambda i: (jnp.maximum(i * hb - 1, 0), 0)),
            pl.BlockSpec((POOL_HALO, d), lambda i: (jnp.minimum((i + 1) * hb, last_hblk), 0)),
            _mod_spec(mod, layer, 1),
            _vec_spec(d, layer),
            _vec_spec(d, layer),
            pl.BlockSpec((None,) + wp.shape[1:], lambda i: (ic, 0, 0, 0)),
            _vec_spec(d, ic),
        ],
        out_specs=pl.BlockSpec((tm, d), lambda i: (i, 0)),
        scratch_shapes=[pltpu.VMEM((tm + 2 * POOL_HALO, d), F32), pltpu.VMEM((tm, d), F32)],
        compiler_params=_cparams(1),
        name="pool_mix",
    )(xu, xu, xu, mod, gpre, gpost, wp, ps)


def kernel(x, c, ctx, c_ctx, w_mod, b_mod, g_mix_pre, g_mix_post, g_mlp_pre, g_mlp_post,
           w_qkv, w_attn_out, lambda_q1, lambda_k1, lambda_q2, lambda_k2, g_subln,
           w_fourier_out, w_pool, pool_scale, w_mlp_in, w_mlp_out):
    n_batch, seq, d = x.shape
    ctx_len = ctx.shape[1]
    depth = w_mod.shape[0]
    n_lat_rows = n_batch * seq
    n_ctx_rows = n_batch * ctx_len
    n_rows = n_lat_rows + n_ctx_rows
    n_heads = w_attn_out.shape[1] // V_HEAD_DIM

    n_cond = -(-(n_batch + 1) // 8) * 8
    cond = jnp.concatenate(
        [c, c_ctx[None, :], jnp.zeros((n_cond - n_batch - 1, d), F32)], axis=0)
    mod = _mod_all(cond, w_mod, b_mod)

    w_qkv_b = w_qkv.astype(BF16)
    w_o_b = w_attn_out.astype(BF16)
    w_f_b = w_fourier_out.astype(BF16)
    w_p_b = w_pool.astype(BF16)
    w1_b = w_mlp_in.astype(BF16)
    w2_b = w_mlp_out.astype(BF16)

    row3 = lambda t: t.reshape(t.shape[0], 1, t.shape[1])
    g_mix_pre, g_mix_post, g_mlp_pre, g_mlp_post = map(
        row3, (g_mix_pre, g_mix_post, g_mlp_pre, g_mlp_post))
    lambda_q1, lambda_k1, lambda_q2, lambda_k2, g_subln, pool_scale = map(
        row3, (lambda_q1, lambda_k1, lambda_q2, lambda_k2, g_subln, pool_scale))
    lam_params = (lambda_q1, lambda_k1, lambda_q2, lambda_k2)

    rope = _rope_tables(seq)
    common = dict(seq=seq, n_batch=n_batch)

    x_l = x.reshape(n_lat_rows, d)
    x_c = ctx.reshape(n_ctx_rows, d)
    xu = None

    ia = ib = ic = 0
    for i in range(depth):
        last = i == depth - 1
        kind = i % N_MIXERS
        out_rows = n_lat_rows if last else n_rows
        if xu is None and kind != 0:
            xu = jnp.concatenate([x_l, x_c], axis=0)
        if kind == 0:
            lam_init = 0.8 - 0.6 * math.exp(-0.3 * i)
            qkw = dict(layer=i, ia=ia, **common)
            akw = dict(ia=ia, lam_init=lam_init, n_batch=n_batch)
            if xu is None:
                qkv_l = _qkv_proj(x_l, mod, g_mix_pre, w_qkv_b, rope, row0=0, **qkw)
                qkv_c = _qkv_proj(x_c, mod, g_mix_pre, w_qkv_b, rope, row0=n_lat_rows, **qkw)
                ctx_row0 = 0
            else:
                qkv_l = qkv_c = _qkv_proj(xu, mod, g_mix_pre, w_qkv_b, rope, row0=0, **qkw)
                ctx_row0 = n_lat_rows
            ctx_seg = (qkv_c, ctx_row0, ctx_len)
            o_l = _attention(qkv_l, 0, seq, [(qkv_l, 0, seq), ctx_seg], lam_params, g_subln,
                             hb=1, **akw)
            o_c = None if last else _attention(
                qkv_c, ctx_row0, ctx_len, [ctx_seg], lam_params, g_subln, hb=n_heads, **akw)
            if xu is None:
                xu = _proj_res(o_l, o_c, x_l, None if last else x_c, mod, g_mix_post, w_o_b,
                               layer=i, iw=ia, **common)
            else:
                xu = _proj_res(o_l, o_c, xu, None, mod, g_mix_post, w_o_b,
                               layer=i, iw=ia, **common)
            ia += 1
        elif kind == 1:
            cg = d // N_FFT_GROUPS
            cc, sc = _dft_tables(cg)
            cs = jnp.concatenate([cc, sc], axis=1).astype(BF16)
            a, b = _fourier_channels(xu, mod, g_mix_pre, cs, layer=i, rows=out_rows, **common)
            tc, ts = _dft_tables(seq)
            y_l = _fourier_seq(a, b, tc.astype(BF16), (-ts).astype(BF16),
                               n_batch=n_batch, length=seq, row0=0)
            y_c = None
            if not last:
                tcc, tsc = _dft_tables(ctx_len)
                y_c = _fourier_seq(a, b, tcc.astype(BF16), (-tsc).astype(BF16),
                                   n_batch=n_batch, length=ctx_len, row0=n_lat_rows)
            xu = _proj_res(y_l, y_c, xu, None, mod, g_mix_post, w_f_b, layer=i, iw=ib, **common)
            ib += 1
        else:
            xu = _pool_mix(xu, mod, g_mix_pre, g_mix_post, w_p_b, pool_scale, layer=i, ic=ic,
                           ctx_len=ctx_len, rows=out_rows, **common)
            ic += 1
        xu = _mlp(xu, mod, g_mlp_pre, g_mlp_post, w1_b, w2_b, layer=i, rows=out_rows, **common)
    return xu[:n_lat_rows].reshape(n_batch, seq, d)
```

```python
import functools
import math

import jax
import jax.numpy as jnp
from jax import lax
from jax.experimental import pallas as pl
from jax.experimental.pallas import tpu as pltpu

F32 = jnp.float32
BF16 = jnp.bfloat16

GRID_W = 64
N_MIXERS = 3
HEAD_DIM = 64
V_HEAD_DIM = 2 * HEAD_DIM
ROPE_BASE = 10000.0
AXIS_ROT = HEAD_DIM // 2
SUBLN_EPS = 1e-5
N_FFT_GROUPS = 4
POOL_WINDOWS = (2, 4, 8, 16)
NORM_EPS = 1e-6
N_MOD = 6
POOL_HALO = 8

LANES = 128
V7X_VMEM_BYTES = 64 * 1024 * 1024
VMEM_LIMIT = V7X_VMEM_BYTES - 8 * 1024 * 1024


def _cparams(n_axes):
    return pltpu.CompilerParams(
        dimension_semantics=("arbitrary",) * n_axes, vmem_limit_bytes=VMEM_LIMIT)


def _tile(pref, *counts):
    t = min(pref, *counts)
    while t > 8 and any(c % t for c in counts):
        t -= 8
    assert t >= 8 and all(c % t == 0 for c in counts), (pref, counts)
    return t


def _rms(x, g, eps):
    return x * lax.rsqrt(jnp.mean(x * x, axis=-1, keepdims=True) + eps) * g


def _mod_row(mod_ref, r, k, d):
    return mod_ref[pl.ds(r, 1), k * d:(k + 1) * d]


def _prenorm(x, g, mod_ref, r, k_shift, d):
    sh = _mod_row(mod_ref, r, k_shift, d)
    sc = _mod_row(mod_ref, r, k_shift + 1, d)
    inv = lax.rsqrt(jnp.mean(x * x, axis=-1, keepdims=True) + NORM_EPS)
    return (x * inv) * (g * (1.0 + sc)) + sh


def _mod_spec(mod, layer, n_axes):
    zeros = (0,) * 2
    return pl.BlockSpec((None,) + mod.shape[1:], lambda *_: (layer,) + zeros)


def _vec_spec(width, index):
    return pl.BlockSpec((None, 1, width), lambda *_: (index, 0, 0))


def _mod_kernel(cond_ref, w_ref, b_ref, o_ref):
    s = jax.nn.silu(cond_ref[...]).astype(BF16)
    o_ref[...] = jnp.dot(s, w_ref[...].astype(BF16), preferred_element_type=F32) + b_ref[...]


def _mod_all(cond, w_mod, b_mod):
    depth, d, n = w_mod.shape
    rows = cond.shape[0]
    tn = _tile(1024, n)
    return pl.pallas_call(
        _mod_kernel,
        out_shape=jax.ShapeDtypeStruct((depth, rows, n), F32),
        grid=(depth, n // tn),
        in_specs=[
            pl.BlockSpec((rows, d), lambda l, j: (0, 0)),
            pl.BlockSpec((None, d, tn), lambda l, j: (l, 0, j)),
            pl.BlockSpec((None, 1, tn), lambda l, j: (l, 0, j)),
        ],
        out_specs=pl.BlockSpec((None, rows, tn), lambda l, j: (l, 0, j)),
        compiler_params=_cparams(2),
        name="ada_mod",
    )(cond, w_mod, b_mod.reshape(depth, 1, n))


def _qkv_kernel(x_ref, mod_ref, g_ref, w_ref, tab_ref, o_ref, u_ref, *, tm, tn, seq, n_batch, row0):
    i = pl.program_id(0)
    d = x_ref.shape[1]

    @pl.when(pl.program_id(1) == 0)
    def _():
        r = jnp.minimum(lax.div(row0 + i * tm, seq), n_batch)
        u_ref[...] = _prenorm(x_ref[...], g_ref[...], mod_ref, r, 0, d).astype(BF16)

    rb = _tile(256, tm)
    for b in range(tm // rb):
        rows = slice(b * rb, (b + 1) * rb)
        y = jnp.dot(u_ref[rows, :], w_ref[...], preferred_element_type=F32)
        cos = tab_ref[0, rows, :]
        sa = tab_ref[1, rows, :]
        sb = tab_ref[2, rows, :]
        for c in range(tn // LANES):
            yc = y[:, c * LANES:(c + 1) * LANES]
            rot = (yc * cos + pltpu.roll(yc, LANES - AXIS_ROT // 2, 1) * sa
                   + pltpu.roll(yc, AXIS_ROT // 2, 1) * sb)
            o_ref[c, rows, :] = rot.astype(BF16)


def _qkv_proj(xa, mod, g, w, rope, *, layer, ia, seq, n_batch, row0):
    rows, d = xa.shape
    n = w.shape[2]
    n_lat_rows = n_batch * seq
    n_lat_here = max(0, min(rows, n_lat_rows - row0))
    tm = _tile(1024, seq, *[c for c in (n_lat_here, rows - n_lat_here) if c])
    tn = _tile(512, n // 3)
    assert tn % LANES == 0 and row0 % tm == 0
    lat_tiles_per_seq = seq // tm
    n_lat_tiles = n_lat_here // tm
    n_q_j = n // 3 // tn

    def tab_index(i, j):
        lat = jnp.where(j < n_q_j, 0, jnp.where(j < 2 * n_q_j, 1, 2))
        ctx = jnp.where(j < n_q_j, 3, 2)
        return (jnp.where(i < n_lat_tiles, lat, ctx), 0, (row0 // tm + i) % lat_tiles_per_seq, 0)

    kern = functools.partial(_qkv_kernel, tm=tm, tn=tn, seq=seq, n_batch=n_batch, row0=row0)
    return pl.pallas_call(
        kern,
        out_shape=jax.ShapeDtypeStruct((n // LANES, rows, LANES), BF16),
        grid=(rows // tm, n // tn),
        in_specs=[
            pl.BlockSpec((tm, d), lambda i, j: (i, 0)),
            _mod_spec(mod, layer, 2),
            _vec_spec(d, layer),
            pl.BlockSpec((None, d, tn), lambda i, j: (ia, 0, j)),
            pl.BlockSpec((None, 3, tm, LANES), tab_index),
        ],
        out_specs=pl.BlockSpec((tn // LANES, tm, LANES), lambda i, j: (j, i, 0)),
        scratch_shapes=[pltpu.VMEM((tm, d), BF16)],
        compiler_params=_cparams(2),
        name="qkv_proj",
    )(xa, mod, g, w, rope)


def _rope_tables(seq):
    rows = seq // GRID_W
    row = jnp.repeat(jnp.arange(rows), GRID_W).astype(F32)
    col = jnp.tile(jnp.arange(GRID_W), rows).astype(F32)
    n_freq = AXIS_ROT // 2
    inv = 1.0 / (ROPE_BASE ** (jnp.arange(n_freq, dtype=F32) / n_freq))
    ang_r = row[:, None] * inv
    ang_c = col[:, None] * inv
    cr, sr, cc, sc = jnp.cos(ang_r), jnp.sin(ang_r), jnp.cos(ang_c), jnp.sin(ang_c)
    z = jnp.zeros_like(sr)
    reps = LANES // HEAD_DIM
    cos = jnp.tile(jnp.concatenate([cr, cr, cc, cc], axis=1), (1, reps))
    sa = jnp.tile(jnp.concatenate([-sr, z, -sc, z], axis=1), (1, reps))
    sb = jnp.tile(jnp.concatenate([z, sr, z, sc], axis=1), (1, reps))
    rot = jnp.stack([cos, sa, sb])
    ident = jnp.stack([jnp.ones_like(cos), jnp.zeros_like(cos), jnp.zeros_like(cos)])
    qs = math.log2(math.e) / math.sqrt(HEAD_DIM)
    return jnp.stack([rot * qs, rot, ident, ident * qs])


def _lambda(lq1_ref, lk1_ref, lq2_ref, lk2_ref, lam_init):
    return (jnp.exp(jnp.sum(lq1_ref[...] * lk1_ref[...], axis=-1, keepdims=True))
            - jnp.exp(jnp.sum(lq2_ref[...] * lk2_ref[...], axis=-1, keepdims=True)) + lam_init)


def _attn_kernel(*refs, n_seg, lam_init, rb):
    q_ref = refs[0]
    seg_refs = refs[1:1 + 2 * n_seg]
    lq1_ref, lk1_ref, lq2_ref, lk2_ref, gs_ref, o_ref, kk_ref, va_ref = refs[1 + 2 * n_seg:]
    hb, q_len, _ = q_ref.shape
    n_blocks = q_len // rb
    lam = _lambda(lq1_ref, lk1_ref, lq2_ref, lk2_ref, lam_init)
    gs = gs_ref[...]
    lane = lax.broadcasted_iota(jnp.int32, (rb, LANES), 1)
    va_ref[:, V_HEAD_DIM:] = jnp.ones((va_ref.shape[0], V_HEAD_DIM), BF16)

    for h in range(hb):
        off = 0
        for s in range(n_seg):
            n = seg_refs[2 * s].shape[1]
            kk_ref[off:off + n, :] = seg_refs[2 * s][h]
            va_ref[off:off + n, 0:V_HEAD_DIM] = seg_refs[2 * s + 1][h]
            off += n

        def scores(r):
            q = q_ref[h, r * rb:(r + 1) * rb, :]
            qq = jnp.concatenate([jnp.where(lane < HEAD_DIM, q, jnp.zeros_like(q)),
                                  jnp.where(lane >= HEAD_DIM, q, jnp.zeros_like(q))], axis=0)
            return lax.dot_general(qq, kk_ref[...], (((1,), (1,)), ((), ())),
                                   preferred_element_type=F32)

        def finish(r, s):
            p = jnp.exp2(s - jnp.max(s, axis=-1, keepdims=True)).astype(BF16)
            acc = jnp.dot(p, va_ref[...], preferred_element_type=F32)
            ratio = acc[:, :V_HEAD_DIM] / acc[:, V_HEAD_DIM:]
            o = ratio[:rb] - lam * ratio[rb:]
            o = _rms(o, gs, SUBLN_EPS) * (1.0 - lam_init)
            o_ref[r * rb:(r + 1) * rb, h * V_HEAD_DIM:(h + 1) * V_HEAD_DIM] = o.astype(BF16)

        s_next = scores(0)
        for r in range(n_blocks):
            s_cur = s_next
            if r + 1 < n_blocks:
                s_next = scores(r + 1)
            finish(r, s_cur)


def _attention(q_arr, q_row0, q_len, segs, lam_params, gs, *, ia, lam_init, n_batch, hb):
    n_heads = q_arr.shape[0] // 3
    assert n_heads % hb == 0 and q_row0 % q_len == 0
    nk = sum(n for _, _, n in segs)
    q_blk0 = q_row0 // q_len
    in_specs = [pl.BlockSpec((hb, q_len, LANES), lambda b, h: (h, q_blk0 + b, 0))]
    args = [q_arr]
    for arr, row0, n in segs:
        assert row0 % n == 0 and arr.shape[0] == 3 * n_heads
        for part in (1, 2):
            in_specs.append(pl.BlockSpec(
                (hb, n, LANES),
                lambda b, h, blk0=row0 // n, col0=part * n_heads // hb: (col0 + h, blk0 + b, 0)))
            args.append(arr)
    in_specs += [_vec_spec(HEAD_DIM, ia)] * 4
    in_specs.append(_vec_spec(V_HEAD_DIM, ia))
    return pl.pallas_call(
        functools.partial(_attn_kernel, n_seg=len(segs), lam_init=lam_init,
                          rb=_tile(128, q_len)),
        out_shape=jax.ShapeDtypeStruct((n_batch * q_len, n_heads * V_HEAD_DIM), BF16),
        grid=(n_batch, n_heads // hb),
        in_specs=in_specs,
        out_specs=pl.BlockSpec((q_len, hb * V_HEAD_DIM), lambda b, h: (b, h)),
        scratch_shapes=[pltpu.VMEM((nk, LANES), BF16), pltpu.VMEM((nk, 2 * V_HEAD_DIM), BF16)],
        compiler_params=_cparams(2),
        name="diff_attn",
    )(*args, *lam_params, gs)


def _proj_res_kernel(*refs, tm, seq, n_batch, n_lat_tiles, has_ctx, split_x):
    refs = list(refs)
    inl_ref = refs.pop(0)
    inc_ref = refs.pop(0) if has_ctx else None
    xl_ref = refs.pop(0)
    xc_ref = refs.pop(0) if split_x else xl_ref
    mod_ref, g_ref, w_ref, o_ref = refs
    i = pl.program_id(0)
    d = o_ref.shape[1]

    def body(inp_ref, x_ref):
        y = jnp.dot(inp_ref[...], w_ref[...], preferred_element_type=F32)
        r = jnp.minimum(lax.div(i * tm, seq), n_batch)
        gate = _mod_row(mod_ref, r, 2, d)
        o_ref[...] = x_ref[...] + gate * _rms(y, g_ref[...], NORM_EPS)

    if has_ctx:
        @pl.when(i < n_lat_tiles)
        def _():
            body(inl_ref, xl_ref)

        @pl.when(i >= n_lat_tiles)
        def _():
            body(inc_ref, xc_ref)
    else:
        body(inl_ref, xl_ref)


def _proj_res(inp_l, inp_c, x_l, x_c, mod, g, w, *, layer, iw, seq, n_batch):
    d = x_l.shape[1]
    n_lat_rows = inp_l.shape[0]
    has_ctx = inp_c is not None
    split_x = x_c is not None
    assert has_ctx or not split_x
    rows = n_lat_rows + (inp_c.shape[0] if has_ctx else 0)
    tm = _tile(512, seq, *([inp_c.shape[0]] if has_ctx else []))
    n_lat_tiles = n_lat_rows // tm
    kin = inp_l.shape[1]
    lat_index = lambda i: (jnp.minimum(i, n_lat_tiles - 1), 0)
    ctx_index = lambda i: (jnp.maximum(i - n_lat_tiles, 0), 0)
    in_specs = [pl.BlockSpec((tm, kin), lat_index)]
    args = [inp_l]
    if has_ctx:
        in_specs.append(pl.BlockSpec((tm, kin), ctx_index))
        args.append(inp_c)
    if split_x:
        in_specs += [pl.BlockSpec((tm, d), lat_index), pl.BlockSpec((tm, d), ctx_index)]
        args += [x_l, x_c]
    else:
        in_specs.append(pl.BlockSpec((tm, d), lambda i: (i, 0)))
        args.append(x_l)
    in_specs += [
        _mod_spec(mod, layer, 1),
        _vec_spec(d, layer),
        pl.BlockSpec((None,) + w.shape[1:], lambda i: (iw, 0, 0)),
    ]
    args += [mod, g, w]
    kern = functools.partial(_proj_res_kernel, tm=tm, seq=seq, n_batch=n_batch,
                             n_lat_tiles=n_lat_tiles, has_ctx=has_ctx, split_x=split_x)
    return pl.pallas_call(
        kern,
        out_shape=jax.ShapeDtypeStruct((rows, d), F32),
        grid=(rows // tm,),
        in_specs=in_specs,
        out_specs=pl.BlockSpec((tm, d), lambda i: (i, 0)),
        compiler_params=_cparams(1),
        name="proj_res",
    )(*args)


def _mlp_kernel(x_ref, mod_ref, gpre_ref, gpost_ref, w1a_ref, w1b_ref, w2a_ref, w2b_ref, o_ref,
                u_ref, h_ref, y_ref, *, tm, seq, n_batch):
    i = pl.program_id(0)
    j = pl.program_id(1)
    d = x_ref.shape[1]
    n1, _, tf = h_ref.shape
    n2, _, tn = y_ref.shape
    r = jnp.minimum(lax.div(i * tm, seq), n_batch)

    @pl.when(j == 0)
    def _():
        u_ref[...] = _prenorm(x_ref[...], gpre_ref[...], mod_ref, r, 3, d).astype(BF16)

    @pl.when(j < n1)
    def _():
        for half, w_ref in enumerate((w1a_ref, w1b_ref)):
            h = jnp.maximum(jnp.dot(u_ref[...], w_ref[...], preferred_element_type=F32), 0.0)
            h_ref[j, :, half * (tf // 2):(half + 1) * (tf // 2)] = (h * h).astype(BF16)

    @pl.when(j >= n1)
    def _():
        acc = None
        for k in range(n1):
            w_ref = w2a_ref if k < n1 // 2 else w2b_ref
            k0 = (k % (n1 // 2)) * tf
            part = jnp.dot(h_ref[k], w_ref[k0:k0 + tf, :], preferred_element_type=F32)
            acc = part if acc is None else acc + part
        y_ref[j - n1] = acc

    @pl.when(j == n1 + n2 - 1)
    def _():
        ss = jnp.sum(y_ref[0] * y_ref[0], axis=-1, keepdims=True)
        for k in range(1, n2):
            ss += jnp.sum(y_ref[k] * y_ref[k], axis=-1, keepdims=True)
        inv = lax.rsqrt(ss / d + NORM_EPS)
        for k in range(n2):
            cols = slice(k * tn, (k + 1) * tn)
            gate = mod_ref[pl.ds(r, 1), 5 * d + k * tn:5 * d + (k + 1) * tn]
            o_ref[:, cols] = x_ref[:, cols] + gate * (y_ref[k] * inv * gpost_ref[:, cols])


def _mlp(xu, mod, gpre, gpost, w1, w2, *, layer, seq, n_batch, rows):
    d = xu.shape[1]
    dff = w1.shape[2]
    tm = _tile(512, seq, *([xu.shape[0] - n_batch * seq] if rows > n_batch * seq else []))
    tf = _tile(1024, dff)
    tn = _tile(256, d)
    n1 = dff // tf
    n2 = d // tn
    assert n1 % 2 == 0 and tf % (2 * LANES) == 0
    kern = functools.partial(_mlp_kernel, tm=tm, seq=seq, n_batch=n_batch)

    def w1_spec(half):
        return pl.BlockSpec((None, d, tf // 2),
                            lambda i, j: (layer, 0, 2 * jnp.minimum(j, n1 - 1) + half))

    def w2_spec(half):
        return pl.BlockSpec((None, dff // 2, tn),
                            lambda i, j: (layer, half, jnp.maximum(j - n1, 0)))

    return pl.pallas_call(
        kern,
        out_shape=jax.ShapeDtypeStruct((rows, d), F32),
        grid=(rows // tm, n1 + n2),
        in_specs=[
            pl.BlockSpec((tm, d), lambda i, j: (i, 0)),
            _mod_spec(mod, layer, 2),
            _vec_spec(d, layer),
            _vec_spec(d, layer),
            w1_spec(0), w1_spec(1), w2_spec(0), w2_spec(1),
        ],
        out_specs=pl.BlockSpec((tm, d), lambda i, j: (i, 0)),
        scratch_shapes=[pltpu.VMEM((tm, d), BF16), pltpu.VMEM((n1, tm, tf), BF16),
                        pltpu.VMEM((n2, tm, tn), F32)],
        compiler_params=_cparams(2),
        name="sq_relu_mlp",
    )(xu, mod, gpre, gpost, w1, w1, w2, w2)


def _dft_tables(n):
    b = max(f for f in range(1, math.isqrt(n) + 1) if n % f == 0)
    a = n // b
    j = jnp.arange(n, dtype=jnp.int32)[:, None]
    ang1 = ((j * jnp.arange(a, dtype=jnp.int32)[None, :]) % a).astype(F32) * (2.0 * math.pi / a)
    ang0 = ((j * jnp.arange(b, dtype=jnp.int32)[None, :]) % n).astype(F32) * (2.0 * math.pi / n)
    c1, s1 = jnp.cos(ang1)[:, :, None], jnp.sin(ang1)[:, :, None]
    c0, s0 = jnp.cos(ang0)[:, None, :], jnp.sin(ang0)[:, None, :]
    scale = 1.0 / math.sqrt(n)
    cos = (c1 * c0 - s1 * s0).reshape(n, n) * scale
    sin = (s1 * c0 + c1 * s0).reshape(n, n) * scale
    return cos, sin


def _fourier_ch_kernel(x_ref, mod_ref, g_ref, cs_ref, a_ref, b_ref, *, tm, seq, n_batch):
    i = pl.program_id(0)
    d = x_ref.shape[1]
    cg = d // N_FFT_GROUPS
    r = jnp.minimum(lax.div(i * tm, seq), n_batch)
    u = _prenorm(x_ref[...], g_ref[...], mod_ref, r, 0, d).astype(BF16)
    for g in range(N_FFT_GROUPS):
        ab = jnp.dot(u[:, g * cg:(g + 1) * cg], cs_ref[...], preferred_element_type=F32)
        a_ref[:, g * cg:(g + 1) * cg] = ab[:, :cg].astype(BF16)
        b_ref[:, g * cg:(g + 1) * cg] = ab[:, cg:].astype(BF16)


def _fourier_channels(xu, mod, g, cs, *, layer, seq, n_batch, rows):
    d = xu.shape[1]
    tm = _tile(512, seq, *([xu.shape[0] - n_batch * seq] if rows > n_batch * seq else []))
    kern = functools.partial(_fourier_ch_kernel, tm=tm, seq=seq, n_batch=n_batch)
    return pl.pallas_call(
        kern,
        out_shape=[jax.ShapeDtypeStruct((rows, d), BF16)] * 2,
        grid=(rows // tm,),
        in_specs=[
            pl.BlockSpec((tm, d), lambda i: (i, 0)),
            _mod_spec(mod, layer, 1),
            _vec_spec(d, layer),
            pl.BlockSpec(cs.shape, lambda i: (0, 0)),
        ],
        out_specs=[pl.BlockSpec((tm, d), lambda i: (i, 0))] * 2,
        compiler_params=_cparams(1),
        name="fourier_channels",
    )(xu, mod, g, cs)


def _fourier_seq_kernel(tc_ref, tms_ref, a_ref, b_ref, o_ref, *, tmm):
    n = tc_ref.shape[0]
    for mi in range(n // tmm):
        rows = slice(mi * tmm, (mi + 1) * tmm)
        y = (jnp.dot(tc_ref[rows, :], a_ref[...], preferred_element_type=F32)
             + jnp.dot(tms_ref[rows, :], b_ref[...], preferred_element_type=F32))
        o_ref[rows, :] = y.astype(BF16)


def _fourier_seq(a, b, tc, tms, *, n_batch, length, row0):
    d = a.shape[1]
    tn = _tile(512, d)
    tmm = _tile(512, length)
    blk0 = row0 // length
    assert row0 % length == 0
    ab_spec = pl.BlockSpec((length, tn), lambda bi, ni: (blk0 + bi, ni))
    tab_spec = pl.BlockSpec((length, length), lambda bi, ni: (0, 0))
    return pl.pallas_call(
        functools.partial(_fourier_seq_kernel, tmm=tmm),
        out_shape=jax.ShapeDtypeStruct((n_batch * length, d), BF16),
        grid=(n_batch, d // tn),
        in_specs=[tab_spec, tab_spec, ab_spec, ab_spec],
        out_specs=pl.BlockSpec((length, tn), lambda bi, ni: (bi, ni)),
        compiler_params=_cparams(2),
        name="fourier_seq",
    )(tc, tms, a, b)


def _pool_kernel(x_ref, xp_ref, xn_ref, mod_ref, gpre_ref, gpost_ref, wp_ref, ps_ref, o_ref,
                 ext_ref, y_ref, *, tm, seq, ctx_len, n_batch, n_lat_rows):
    i = pl.program_id(0)
    d = x_ref.shape[1]
    cg = d // len(POOL_WINDOWS)
    row0 = i * tm
    is_lat = row0 < n_lat_rows
    slen = jnp.where(is_lat, seq, ctx_len)
    pos0 = jnp.where(is_lat, lax.rem(row0, seq), lax.rem(row0 - n_lat_rows, ctx_len))
    r = jnp.minimum(lax.div(row0, seq), n_batch)
    g_pre = gpre_ref[...]
    keep_prev = jnp.where(pos0 > 0, 1.0, 0.0).astype(F32)
    keep_next = jnp.where(pos0 + tm < slen, 1.0, 0.0).astype(F32)
    ext_ref[0:POOL_HALO, :] = _prenorm(xp_ref[...], g_pre, mod_ref, r, 0, d) * keep_prev
    ext_ref[POOL_HALO:POOL_HALO + tm, :] = _prenorm(x_ref[...], g_pre, mod_ref, r, 0, d)
    ext_ref[POOL_HALO + tm:, :] = _prenorm(xn_ref[...], g_pre, mod_ref, r, 0, d) * keep_next

    pos = pos0 + lax.broadcasted_iota(jnp.int32, (tm, 1), 0)
    for g, w in enumerate(POOL_WINDOWS):
        cols = slice(g * cg, (g + 1) * cg)
        acc = ext_ref[POOL_HALO - w // 2:POOL_HALO - w // 2 + tm, cols]
        for dlt in range(-w // 2 + 1, w // 2):
            acc = acc + ext_ref[POOL_HALO + dlt:POOL_HALO + dlt + tm, cols]
        lo = jnp.maximum(pos - w // 2, 0)
        hi = jnp.minimum(pos - w // 2 + w, slen)
        mean = acc / (hi - lo).astype(F32)
        diff = (mean - ext_ref[POOL_HALO:POOL_HALO + tm, cols]).astype(BF16)
        y_ref[:, cols] = jnp.dot(diff, wp_ref[g], preferred_element_type=F32) * ps_ref[:, cols]

    gate = _mod_row(mod_ref, r, 2, d)
    o_ref[...] = x_ref[...] + gate * _rms(y_ref[...], gpost_ref[...], NORM_EPS)


def _pool_mix(xu, mod, gpre, gpost, wp, ps, *, layer, ic, seq, ctx_len, n_batch, rows):
    d = xu.shape[1]
    n_lat_rows = n_batch * seq
    tm = _tile(256, seq, *([ctx_len] if rows > n_lat_rows else []))
    assert tm % POOL_HALO == 0 and POOL_HALO >= max(POOL_WINDOWS) // 2
    hb = tm // POOL_HALO
    last_hblk = xu.shape[0] // POOL_HALO - 1
    kern = functools.partial(_pool_kernel, tm=tm, seq=seq, ctx_len=ctx_len, n_batch=n_batch,
                             n_lat_rows=n_lat_rows)
    return pl.pallas_call(
        kern,
        out_shape=jax.ShapeDtypeStruct((rows, d), F32),
        grid=(rows // tm,),
        in_specs=[
            pl.BlockSpec((tm, d), lambda i: (i, 0)),
            pl.BlockSpec((POOL_HALO, d), lambda i: (jnp.maximum(i * hb - 1, 0), 0)),
            pl.BlockSpec((POOL_HALO, d), lambda i: (jnp.minimum((i + 1) * hb, last_hblk), 0)),
            _mod_spec(mod, layer, 1),
            _vec_spec(d, layer),
            _vec_spec(d, layer),
            pl.BlockSpec((None,) + wp.shape[1:], lambda i: (ic, 0, 0, 0)),
            _vec_spec(d, ic),
        ],
        out_specs=pl.BlockSpec((tm, d), lambda i: (i, 0)),
        scratch_shapes=[pltpu.VMEM((tm + 2 * POOL_HALO, d), F32), pltpu.VMEM((tm, d), F32)],
        compiler_params=_cparams(1),
        name="pool_mix",
    )(xu, xu, xu, mod, gpre, gpost, wp, ps)


def kernel(x, c, ctx, c_ctx, w_mod, b_mod, g_mix_pre, g_mix_post, g_mlp_pre, g_mlp_post,
           w_qkv, w_attn_out, lambda_q1, lambda_k1, lambda_q2, lambda_k2, g_subln,
           w_fourier_out, w_pool, pool_scale, w_mlp_in, w_mlp_out):
    n_batch, seq, d = x.shape
    ctx_len = ctx.shape[1]
    depth = w_mod.shape[0]
    n_lat_rows = n_batch * seq
    n_ctx_rows = n_batch * ctx_len
    n_rows = n_lat_rows + n_ctx_rows
    n_heads = w_attn_out.shape[1] // V_HEAD_DIM

    n_cond = -(-(n_batch + 1) // 8) * 8
    cond = jnp.concatenate(
        [c, c_ctx[None, :], jnp.zeros((n_cond - n_batch - 1, d), F32)], axis=0)
    mod = _mod_all(cond, w_mod, b_mod)

    w_qkv_b = w_qkv.astype(BF16)
    w_o_b = w_attn_out.astype(BF16)
    w_f_b = w_fourier_out.astype(BF16)
    w_p_b = w_pool.astype(BF16)
    w1_b = w_mlp_in.astype(BF16)
    w2_b = w_mlp_out.astype(BF16)

    row3 = lambda t: t.reshape(t.shape[0], 1, t.shape[1])
    g_mix_pre, g_mix_post, g_mlp_pre, g_mlp_post = map(
        row3, (g_mix_pre, g_mix_post, g_mlp_pre, g_mlp_post))
    lambda_q1, lambda_k1, lambda_q2, lambda_k2, g_subln, pool_scale = map(
        row3, (lambda_q1, lambda_k1, lambda_q2, lambda_k2, g_subln, pool_scale))
    lam_params = (lambda_q1, lambda_k1, lambda_q2, lambda_k2)

    rope = _rope_tables(seq)
    common = dict(seq=seq, n_batch=n_batch)

    x_l = x.reshape(n_lat_rows, d)
    x_c = ctx.reshape(n_ctx_rows, d)
    xu = None

    ia = ib = ic = 0
    for i in range(depth):
        last = i == depth - 1
        kind = i % N_MIXERS
        out_rows = n_lat_rows if last else n_rows
        if xu is None and kind != 0:
            xu = jnp.concatenate([x_l, x_c], axis=0)
        if kind == 0:
            lam_init = 0.8 - 0.6 * math.exp(-0.3 * i)
            qkw = dict(layer=i, ia=ia, **common)
            akw = dict(ia=ia, lam_init=lam_init, n_batch=n_batch)
            if xu is None:
                qkv_l = _qkv_proj(x_l, mod, g_mix_pre, w_qkv_b, rope, row0=0, **qkw)
                qkv_c = _qkv_proj(x_c, mod, g_mix_pre, w_qkv_b, rope, row0=n_lat_rows, **qkw)
                ctx_row0 = 0
            else:
                qkv_l = qkv_c = _qkv_proj(xu, mod, g_mix_pre, w_qkv_b, rope, row0=0, **qkw)
                ctx_row0 = n_lat_rows
            ctx_seg = (qkv_c, ctx_row0, ctx_len)
            o_l = _attention(qkv_l, 0, seq, [(qkv_l, 0, seq), ctx_seg], lam_params, g_subln,
                             hb=1, **akw)
            o_c = None if last else _attention(
                qkv_c, ctx_row0, ctx_len, [ctx_seg], lam_params, g_subln, hb=n_heads, **akw)
            if xu is None:
                xu = _proj_res(o_l, o_c, x_l, None if last else x_c, mod, g_mix_post, w_o_b,
                               layer=i, iw=ia, **common)
            else:
                xu = _proj_res(o_l, o_c, xu, None, mod, g_mix_post, w_o_b,
                               layer=i, iw=ia, **common)
            ia += 1
        elif kind == 1:
            cg = d // N_FFT_GROUPS
            cc, sc = _dft_tables(cg)
            cs = jnp.concatenate([cc, sc], axis=1).astype(BF16)
            a, b = _fourier_channels(xu, mod, g_mix_pre, cs, layer=i, rows=out_rows, **common)
            tc, ts = _dft_tables(seq)
            y_l = _fourier_seq(a, b, tc.astype(BF16), (-ts).astype(BF16),
                               n_batch=n_batch, length=seq, row0=0)
            y_c = None
            if not last:
                tcc, tsc = _dft_tables(ctx_len)
                y_c = _fourier_seq(a, b, tcc.astype(BF16), (-tsc).astype(BF16),
                                   n_batch=n_batch, length=ctx_len, row0=n_lat_rows)
            xu = _proj_res(y_l, y_c, xu, None, mod, g_mix_post, w_f_b, layer=i, iw=ib, **common)
            ib += 1
        else:
            xu = _pool_mix(xu, mod, g_mix_pre, g_mix_post, w_p_b, pool_scale, layer=i, ic=ic,
                           ctx_len=ctx_len, rows=out_rows, **common)
            ic += 1
        xu = _mlp(xu, mod, g_mlp_pre, g_mlp_post, w1_b, w2_b, layer=i, rows=out_rows, **common)
    return xu[:n_lat_rows].reshape(n_batch, seq, d)
```

```python
import functools
import math

import jax
import jax.numpy as jnp
from jax import lax
from jax.experimental import pallas as pl
from jax.experimental.pallas import tpu as pltpu

F32 = jnp.float32
BF16 = jnp.bfloat16

GRID_W = 64
N_MIXERS = 3
HEAD_DIM = 64
V_HEAD_DIM = 2 * HEAD_DIM
ROPE_BASE = 10000.0
AXIS_ROT = HEAD_DIM // 2
SUBLN_EPS = 1e-5
N_FFT_GROUPS = 4
POOL_WINDOWS = (2, 4, 8, 16)
NORM_EPS = 1e-6
N_MOD = 6
POOL_HALO = 8

LANES = 128
V7X_VMEM_BYTES = 64 * 1024 * 1024
VMEM_LIMIT = V7X_VMEM_BYTES - 8 * 1024 * 1024


def _cparams(n_axes):
    return pltpu.CompilerParams(
        dimension_semantics=("arbitrary",) * n_axes, vmem_limit_bytes=VMEM_LIMIT)


def _tile(pref, *counts):
    t = min(pref, *counts)
    while t > 8 and any(c % t for c in counts):
        t -= 8
    assert t >= 8 and all(c % t == 0 for c in counts), (pref, counts)
    return t


def _rms(x, g, eps):
    return x * lax.rsqrt(jnp.mean(x * x, axis=-1, keepdims=True) + eps) * g


def _mod_row(mod_ref, r, k, d):
    return mod_ref[pl.ds(r, 1), k * d:(k + 1) * d]


def _prenorm(x, g, mod_ref, r, k_shift, d):
    sh = _mod_row(mod_ref, r, k_shift, d)
    sc = _mod_row(mod_ref, r, k_shift + 1, d)
    inv = lax.rsqrt(jnp.mean(x * x, axis=-1, keepdims=True) + NORM_EPS)
    return (x * inv) * (g * (1.0 + sc)) + sh


def _mod_spec(mod, layer, n_axes):
    zeros = (0,) * 2
    return pl.BlockSpec((None,) + mod.shape[1:], lambda *_: (layer,) + zeros)


def _vec_spec(width, index):
    return pl.BlockSpec((None, 1, width), lambda *_: (index, 0, 0))


def _mod_kernel(cond_ref, w_ref, b_ref, o_ref):
    s = jax.nn.silu(cond_ref[...]).astype(BF16)
    o_ref[...] = jnp.dot(s, w_ref[...].astype(BF16), preferred_element_type=F32) + b_ref[...]


def _mod_all(cond, w_mod, b_mod):
    depth, d, n = w_mod.shape
    rows = cond.shape[0]
    tn = _tile(1024, n)
    return pl.pallas_call(
        _mod_kernel,
        out_shape=jax.ShapeDtypeStruct((depth, rows, n), F32),
        grid=(depth, n // tn),
        in_specs=[
            pl.BlockSpec((rows, d), lambda l, j: (0, 0)),
            pl.BlockSpec((None, d, tn), lambda l, j: (l, 0, j)),
            pl.BlockSpec((None, 1, tn), lambda l, j: (l, 0, j)),
        ],
        out_specs=pl.BlockSpec((None, rows, tn), lambda l, j: (l, 0, j)),
        compiler_params=_cparams(2),
        name="ada_mod",
    )(cond, w_mod, b_mod.reshape(depth, 1, n))


def _qkv_kernel(x_ref, mod_ref, g_ref, w_ref, tab_ref, o_ref, u_ref, *, tm, tn, seq, n_batch, row0):
    i = pl.program_id(0)
    d = x_ref.shape[1]

    @pl.when(pl.program_id(1) == 0)
    def _():
        r = jnp.minimum(lax.div(row0 + i * tm, seq), n_batch)
        u_ref[...] = _prenorm(x_ref[...], g_ref[...], mod_ref, r, 0, d).astype(BF16)

    rb = _tile(256, tm)
    w = w_ref[...].astype(BF16)
    for b in range(tm // rb):
        rows = slice(b * rb, (b + 1) * rb)
        y = jnp.dot(u_ref[rows, :], w, preferred_element_type=F32)
        cos = tab_ref[0, rows, :]
        sa = tab_ref[1, rows, :]
        sb = tab_ref[2, rows, :]
        for c in range(tn // LANES):
            yc = y[:, c * LANES:(c + 1) * LANES]
            rot = (yc * cos + pltpu.roll(yc, LANES - AXIS_ROT // 2, 1) * sa
                   + pltpu.roll(yc, AXIS_ROT // 2, 1) * sb)
            o_ref[c, rows, :] = rot.astype(BF16)


def _qkv_proj(xa, mod, g, w, rope, *, layer, ia, seq, n_batch, row0):
    rows, d = xa.shape
    n = w.shape[2]
    n_lat_rows = n_batch * seq
    n_lat_here = max(0, min(rows, n_lat_rows - row0))
    tm = _tile(1024, seq, *[c for c in (n_lat_here, rows - n_lat_here) if c])
    tn = _tile(512, n // 3)
    assert tn % LANES == 0 and row0 % tm == 0
    lat_tiles_per_seq = seq // tm
    n_lat_tiles = n_lat_here // tm
    n_q_j = n // 3 // tn

    def tab_index(i, j):
        lat = jnp.where(j < n_q_j, 0, jnp.where(j < 2 * n_q_j, 1, 2))
        ctx = jnp.where(j < n_q_j, 3, 2)
        return (jnp.where(i < n_lat_tiles, lat, ctx), 0, (row0 // tm + i) % lat_tiles_per_seq, 0)

    kern = functools.partial(_qkv_kernel, tm=tm, tn=tn, seq=seq, n_batch=n_batch, row0=row0)
    return pl.pallas_call(
        kern,
        out_shape=jax.ShapeDtypeStruct((n // LANES, rows, LANES), BF16),
        grid=(rows // tm, n // tn),
        in_specs=[
            pl.BlockSpec((tm, d), lambda i, j: (i, 0)),
            _mod_spec(mod, layer, 2),
            _vec_spec(d, layer),
            pl.BlockSpec((None, d, tn), lambda i, j: (ia, 0, j)),
            pl.BlockSpec((None, 3, tm, LANES), tab_index),
        ],
        out_specs=pl.BlockSpec((tn // LANES, tm, LANES), lambda i, j: (j, i, 0)),
        scratch_shapes=[pltpu.VMEM((tm, d), BF16)],
        compiler_params=_cparams(2),
        name="qkv_proj",
    )(xa, mod, g, w, rope)


def _rope_tables(seq):
    rows = seq // GRID_W
    row = jnp.repeat(jnp.arange(rows), GRID_W).astype(F32)
    col = jnp.tile(jnp.arange(GRID_W), rows).astype(F32)
    n_freq = AXIS_ROT // 2
    inv = 1.0 / (ROPE_BASE ** (jnp.arange(n_freq, dtype=F32) / n_freq))
    ang_r = row[:, None] * inv
    ang_c = col[:, None] * inv
    cr, sr, cc, sc = jnp.cos(ang_r), jnp.sin(ang_r), jnp.cos(ang_c), jnp.sin(ang_c)
    z = jnp.zeros_like(sr)
    reps = LANES // HEAD_DIM
    cos = jnp.tile(jnp.concatenate([cr, cr, cc, cc], axis=1), (1, reps))
    sa = jnp.tile(jnp.concatenate([-sr, z, -sc, z], axis=1), (1, reps))
    sb = jnp.tile(jnp.concatenate([z, sr, z, sc], axis=1), (1, reps))
    rot = jnp.stack([cos, sa, sb])
    ident = jnp.stack([jnp.ones_like(cos), jnp.zeros_like(cos), jnp.zeros_like(cos)])
    qs = math.log2(math.e) / math.sqrt(HEAD_DIM)
    return jnp.stack([rot * qs, rot, ident, ident * qs])


def _lambda(lq1_ref, lk1_ref, lq2_ref, lk2_ref, lam_init):
    return (jnp.exp(jnp.sum(lq1_ref[...] * lk1_ref[...], axis=-1, keepdims=True))
            - jnp.exp(jnp.sum(lq2_ref[...] * lk2_ref[...], axis=-1, keepdims=True)) + lam_init)


def _attn_kernel(*refs, n_seg, lam_init, rb):
    q_ref = refs[0]
    seg_refs = refs[1:1 + 2 * n_seg]
    lq1_ref, lk1_ref, lq2_ref, lk2_ref, gs_ref, o_ref, kk_ref, va_ref = refs[1 + 2 * n_seg:]
    hb, q_len, _ = q_ref.shape
    n_blocks = q_len // rb
    lam = _lambda(lq1_ref, lk1_ref, lq2_ref, lk2_ref, lam_init)
    gs = gs_ref[...]
    lane = lax.broadcasted_iota(jnp.int32, (rb, LANES), 1)
    va_ref[:, V_HEAD_DIM:] = jnp.ones((va_ref.shape[0], V_HEAD_DIM), BF16)

    for h in range(hb):
        off = 0
        for s in range(n_seg):
            n = seg_refs[2 * s].shape[1]
            kk_ref[off:off + n, :] = seg_refs[2 * s][h]
            va_ref[off:off + n, 0:V_HEAD_DIM] = seg_refs[2 * s + 1][h]
            off += n

        def scores(r):
            q = q_ref[h, r * rb:(r + 1) * rb, :]
            qq = jnp.concatenate([jnp.where(lane < HEAD_DIM, q, jnp.zeros_like(q)),
                                  jnp.where(lane >= HEAD_DIM, q, jnp.zeros_like(q))], axis=0)
            return lax.dot_general(qq, kk_ref[...], (((1,), (1,)), ((), ())),
                                   preferred_element_type=F32)

        def finish(r, s):
            p = jnp.exp2(s - jnp.max(s, axis=-1, keepdims=True)).astype(BF16)
            acc = jnp.dot(p, va_ref[...], preferred_element_type=F32)
            ratio = acc[:, :V_HEAD_DIM] / acc[:, V_HEAD_DIM:]
            o = ratio[:rb] - lam * ratio[rb:]
            o = _rms(o, gs, SUBLN_EPS) * (1.0 - lam_init)
            o_ref[r * rb:(r + 1) * rb, h * V_HEAD_DIM:(h + 1) * V_HEAD_DIM] = o.astype(BF16)

        s_next = scores(0)
        for r in range(n_blocks):
            s_cur = s_next
            if r + 1 < n_blocks:
                s_next = scores(r + 1)
            finish(r, s_cur)


def _attention(q_arr, q_row0, q_len, segs, lam_params, gs, *, ia, lam_init, n_batch, hb):
    n_heads = q_arr.shape[0] // 3
    assert n_heads % hb == 0 and q_row0 % q_len == 0
    nk = sum(n for _, _, n in segs)
    q_blk0 = q_row0 // q_len
    in_specs = [pl.BlockSpec((hb, q_len, LANES), lambda b, h: (h, q_blk0 + b, 0))]
    args = [q_arr]
    for arr, row0, n in segs:
        assert row0 % n == 0 and arr.shape[0] == 3 * n_heads
        for part in (1, 2):
            in_specs.append(pl.BlockSpec(
                (hb, n, LANES),
                lambda b, h, blk0=row0 // n, col0=part * n_heads // hb: (col0 + h, blk0 + b, 0)))
            args.append(arr)
    in_specs += [_vec_spec(HEAD_DIM, ia)] * 4
    in_specs.append(_vec_spec(V_HEAD_DIM, ia))
    return pl.pallas_call(
        functools.partial(_attn_kernel, n_seg=len(segs), lam_init=lam_init,
                          rb=_tile(128, q_len)),
        out_shape=jax.ShapeDtypeStruct((n_batch * q_len, n_heads * V_HEAD_DIM), BF16),
        grid=(n_batch, n_heads // hb),
        in_specs=in_specs,
        out_specs=pl.BlockSpec((q_len, hb * V_HEAD_DIM), lambda b, h: (b, h)),
        scratch_shapes=[pltpu.VMEM((nk, LANES), BF16), pltpu.VMEM((nk, 2 * V_HEAD_DIM), BF16)],
        compiler_params=_cparams(2),
        name="diff_attn",
    )(*args, *lam_params, gs)


def _proj_res_kernel(*refs, tm, seq, n_batch, n_lat_tiles, has_ctx, split_x):
    refs = list(refs)
    inl_ref = refs.pop(0)
    inc_ref = refs.pop(0) if has_ctx else None
    xl_ref = refs.pop(0)
    xc_ref = refs.pop(0) if split_x else xl_ref
    mod_ref, g_ref, w_ref, o_ref = refs
    i = pl.program_id(0)
    d = o_ref.shape[1]

    def body(inp_ref, x_ref):
        y = jnp.dot(inp_ref[...], w_ref[...], preferred_element_type=F32)
        r = jnp.minimum(lax.div(i * tm, seq), n_batch)
        gate = _mod_row(mod_ref, r, 2, d)
        o_ref[...] = x_ref[...] + gate * _rms(y, g_ref[...], NORM_EPS)

    if has_ctx:
        @pl.when(i < n_lat_tiles)
        def _():
            body(inl_ref, xl_ref)

        @pl.when(i >= n_lat_tiles)
        def _():
            body(inc_ref, xc_ref)
    else:
        body(inl_ref, xl_ref)


def _proj_res(inp_l, inp_c, x_l, x_c, mod, g, w, *, layer, iw, seq, n_batch):
    d = x_l.shape[1]
    n_lat_rows = inp_l.shape[0]
    has_ctx = inp_c is not None
    split_x = x_c is not None
    assert has_ctx or not split_x
    rows = n_lat_rows + (inp_c.shape[0] if has_ctx else 0)
    tm = _tile(512, seq, *([inp_c.shape[0]] if has_ctx else []))
    n_lat_tiles = n_lat_rows // tm
    kin = inp_l.shape[1]
    lat_index = lambda i: (jnp.minimum(i, n_lat_tiles - 1), 0)
    ctx_index = lambda i: (jnp.maximum(i - n_lat_tiles, 0), 0)
    in_specs = [pl.BlockSpec((tm, kin), lat_index)]
    args = [inp_l]
    if has_ctx:
        in_specs.append(pl.BlockSpec((tm, kin), ctx_index))
        args.append(inp_c)
    if split_x:
        in_specs += [pl.BlockSpec((tm, d), lat_index), pl.BlockSpec((tm, d), ctx_index)]
        args += [x_l, x_c]
    else:
        in_specs.append(pl.BlockSpec((tm, d), lambda i: (i, 0)))
        args.append(x_l)
    in_specs += [
        _mod_spec(mod, layer, 1),
        _vec_spec(d, layer),
        pl.BlockSpec((None,) + w.shape[1:], lambda i: (iw, 0, 0)),
    ]
    args += [mod, g, w]
    kern = functools.partial(_proj_res_kernel, tm=tm, seq=seq, n_batch=n_batch,
                             n_lat_tiles=n_lat_tiles, has_ctx=has_ctx, split_x=split_x)
    return pl.pallas_call(
        kern,
        out_shape=jax.ShapeDtypeStruct((rows, d), F32),
        grid=(rows // tm,),
        in_specs=in_specs,
        out_specs=pl.BlockSpec((tm, d), lambda i: (i, 0)),
        compiler_params=_cparams(1),
        name="proj_res",
    )(*args)


def _mlp_kernel(*refs, tm, seq, n_batch, cast_next):
    if cast_next:
        (x_ref, mod_ref, gpre_ref, gpost_ref, w1_ref, w2_ref, w1n_ref, w2n_ref,
         o_ref, w1o_ref, w2o_ref, u_ref, h_ref, y_ref) = refs
    else:
        x_ref, mod_ref, gpre_ref, gpost_ref, w1_ref, w2_ref, o_ref, u_ref, h_ref, y_ref = refs
    i = pl.program_id(0)
    j = pl.program_id(1)
    d = x_ref.shape[1]
    n1, _, tf = h_ref.shape
    n2, _, tn = y_ref.shape
    r = jnp.minimum(lax.div(i * tm, seq), n_batch)

    def cast_blocks():
        if cast_next:
            w1o_ref[...] = w1n_ref[...].astype(BF16)
            w2o_ref[...] = w2n_ref[...].astype(BF16)

    @pl.when(j == 0)
    def _():
        u_ref[...] = _prenorm(x_ref[...], gpre_ref[...], mod_ref, r, 3, d).astype(BF16)

    @pl.when(j < n1)
    def _():
        h = jnp.maximum(jnp.dot(u_ref[...], w1_ref[...], preferred_element_type=F32), 0.0)
        h_ref[j] = (h * h).astype(BF16)
        cast_blocks()

    @pl.when(j >= n1)
    def _():
        acc = jnp.dot(h_ref[0], w2_ref[0:tf, :], preferred_element_type=F32)
        for k in range(1, n1):
            acc += jnp.dot(h_ref[k], w2_ref[k * tf:(k + 1) * tf, :], preferred_element_type=F32)
        y_ref[j - n1] = acc
        cast_blocks()

    @pl.when(j == n1 + n2 - 1)
    def _():
        ss = jnp.sum(y_ref[0] * y_ref[0], axis=-1, keepdims=True)
        for k in range(1, n2):
            ss += jnp.sum(y_ref[k] * y_ref[k], axis=-1, keepdims=True)
        inv = lax.rsqrt(ss / d + NORM_EPS)
        for k in range(n2):
            cols = slice(k * tn, (k + 1) * tn)
            gate = mod_ref[pl.ds(r, 1), 5 * d + k * tn:5 * d + (k + 1) * tn]
            o_ref[:, cols] = x_ref[:, cols] + gate * (y_ref[k] * inv * gpost_ref[:, cols])


CAST_BLOCK_BYTES = 1024 * 1024


def _mlp(xu, mod, gpre, gpost, w1, w2, w_next, *, layer, seq, n_batch, rows):
    d = xu.shape[1]
    dff = w1.shape[1]
    tm = _tile(512, seq, *([xu.shape[0] - n_batch * seq] if rows > n_batch * seq else []))
    tf = _tile(1024, dff)
    tn = _tile(256, d)
    n1 = dff // tf
    n2 = d // tn
    n_steps = (rows // tm) * (n1 + n2)
    cast_next = w_next is not None
    kern = functools.partial(_mlp_kernel, tm=tm, seq=seq, n_batch=n_batch, cast_next=cast_next)
    in_specs = [
        pl.BlockSpec((tm, d), lambda i, j: (i, 0)),
        _mod_spec(mod, layer, 2),
        _vec_spec(d, layer),
        _vec_spec(d, layer),
        pl.BlockSpec((d, tf), lambda i, j: (0, jnp.minimum(j, n1 - 1))),
        pl.BlockSpec((dff, tn), lambda i, j: (0, jnp.maximum(j - n1, 0))),
    ]
    args = [xu, mod, gpre, gpost, w1, w2]
    out_shape = [jax.ShapeDtypeStruct((rows, d), F32)]
    out_specs = [pl.BlockSpec((tm, d), lambda i, j: (i, 0))]
    if cast_next:
        r1 = _tile(CAST_BLOCK_BYTES // (4 * dff), d)
        r2 = _tile(CAST_BLOCK_BYTES // (4 * d), dff)
        c1, c2 = d // r1, dff // r2
        assert c1 + c2 <= n_steps

        def blk1(i, j):
            return jnp.minimum(i * (n1 + n2) + j, c1 - 1)

        def blk2(i, j):
            return jnp.clip(i * (n1 + n2) + j - c1, 0, c2 - 1)

        in_specs += [pl.BlockSpec((None, r1, dff), lambda i, j: (layer + 1, blk1(i, j), 0)),
                     pl.BlockSpec((None, r2, d), lambda i, j: (layer + 1, blk2(i, j), 0))]
        args += list(w_next)
        out_shape += [jax.ShapeDtypeStruct((d, dff), BF16), jax.ShapeDtypeStruct((dff, d), BF16)]
        out_specs += [pl.BlockSpec((r1, dff), lambda i, j: (blk1(i, j), 0)),
                      pl.BlockSpec((r2, d), lambda i, j: (blk2(i, j), 0))]
    outs = pl.pallas_call(
        kern,
        out_shape=out_shape,
        grid=(rows // tm, n1 + n2),
        in_specs=in_specs,
        out_specs=out_specs,
        scratch_shapes=[pltpu.VMEM((tm, d), BF16), pltpu.VMEM((n1, tm, tf), BF16),
                        pltpu.VMEM((n2, tm, tn), F32)],
        compiler_params=_cparams(2),
        name="sq_relu_mlp",
    )(*args)
    return outs if cast_next else (outs[0], None, None)


def _dft_tables(n):
    b = max(f for f in range(1, math.isqrt(n) + 1) if n % f == 0)
    a = n // b
    j = jnp.arange(n, dtype=jnp.int32)[:, None]
    ang1 = ((j * jnp.arange(a, dtype=jnp.int32)[None, :]) % a).astype(F32) * (2.0 * math.pi / a)
    ang0 = ((j * jnp.arange(b, dtype=jnp.int32)[None, :]) % n).astype(F32) * (2.0 * math.pi / n)
    c1, s1 = jnp.cos(ang1)[:, :, None], jnp.sin(ang1)[:, :, None]
    c0, s0 = jnp.cos(ang0)[:, None, :], jnp.sin(ang0)[:, None, :]
    scale = 1.0 / math.sqrt(n)
    cos = (c1 * c0 - s1 * s0).reshape(n, n) * scale
    sin = (s1 * c0 + c1 * s0).reshape(n, n) * scale
    return cos, sin


def _fourier_ch_kernel(x_ref, mod_ref, g_ref, cs_ref, a_ref, b_ref, *, tm, seq, n_batch):
    i = pl.program_id(0)
    d = x_ref.shape[1]
    cg = d // N_FFT_GROUPS
    r = jnp.minimum(lax.div(i * tm, seq), n_batch)
    u = _prenorm(x_ref[...], g_ref[...], mod_ref, r, 0, d).astype(BF16)
    for g in range(N_FFT_GROUPS):
        ab = jnp.dot(u[:, g * cg:(g + 1) * cg], cs_ref[...], preferred_element_type=F32)
        a_ref[:, g * cg:(g + 1) * cg] = ab[:, :cg].astype(BF16)
        b_ref[:, g * cg:(g + 1) * cg] = ab[:, cg:].astype(BF16)


def _fourier_channels(xu, mod, g, cs, *, layer, seq, n_batch, rows):
    d = xu.shape[1]
    tm = _tile(512, seq, *([xu.shape[0] - n_batch * seq] if rows > n_batch * seq else []))
    kern = functools.partial(_fourier_ch_kernel, tm=tm, seq=seq, n_batch=n_batch)
    return pl.pallas_call(
        kern,
        out_shape=[jax.ShapeDtypeStruct((rows, d), BF16)] * 2,
        grid=(rows // tm,),
        in_specs=[
            pl.BlockSpec((tm, d), lambda i: (i, 0)),
            _mod_spec(mod, layer, 1),
            _vec_spec(d, layer),
            pl.BlockSpec(cs.shape, lambda i: (0, 0)),
        ],
        out_specs=[pl.BlockSpec((tm, d), lambda i: (i, 0))] * 2,
        compiler_params=_cparams(1),
        name="fourier_channels",
    )(xu, mod, g, cs)


def _fourier_seq_kernel(tc_ref, tms_ref, a_ref, b_ref, o_ref, *, tmm):
    n = tc_ref.shape[0]
    for mi in range(n // tmm):
        rows = slice(mi * tmm, (mi + 1) * tmm)
        y = (jnp.dot(tc_ref[rows, :], a_ref[...], preferred_element_type=F32)
             + jnp.dot(tms_ref[rows, :], b_ref[...], preferred_element_type=F32))
        o_ref[rows, :] = y.astype(BF16)


def _fourier_seq(a, b, tc, tms, *, n_batch, length, row0):
    d = a.shape[1]
    tn = _tile(512, d)
    tmm = _tile(512, length)
    blk0 = row0 // length
    assert row0 % length == 0
    ab_spec = pl.BlockSpec((length, tn), lambda bi, ni: (blk0 + bi, ni))
    tab_spec = pl.BlockSpec((length, length), lambda bi, ni: (0, 0))
    return pl.pallas_call(
        functools.partial(_fourier_seq_kernel, tmm=tmm),
        out_shape=jax.ShapeDtypeStruct((n_batch * length, d), BF16),
        grid=(n_batch, d // tn),
        in_specs=[tab_spec, tab_spec, ab_spec, ab_spec],
        out_specs=pl.BlockSpec((length, tn), lambda bi, ni: (bi, ni)),
        compiler_params=_cparams(2),
        name="fourier_seq",
    )(tc, tms, a, b)


def _pool_kernel(x_ref, xp_ref, xn_ref, mod_ref, gpre_ref, gpost_ref, wp_ref, ps_ref, o_ref,
                 ext_ref, y_ref, *, tm, seq, ctx_len, n_batch, n_lat_rows):
    i = pl.program_id(0)
    d = x_ref.shape[1]
    cg = d // len(POOL_WINDOWS)
    row0 = i * tm
    is_lat = row0 < n_lat_rows
    slen = jnp.where(is_lat, seq, ctx_len)
    pos0 = jnp.where(is_lat, lax.rem(row0, seq), lax.rem(row0 - n_lat_rows, ctx_len))
    r = jnp.minimum(lax.div(row0, seq), n_batch)
    g_pre = gpre_ref[...]
    keep_prev = jnp.where(pos0 > 0, 1.0, 0.0).astype(F32)
    keep_next = jnp.where(pos0 + tm < slen, 1.0, 0.0).astype(F32)
    ext_ref[0:POOL_HALO, :] = _prenorm(xp_ref[...], g_pre, mod_ref, r, 0, d) * keep_prev
    ext_ref[POOL_HALO:POOL_HALO + tm, :] = _prenorm(x_ref[...], g_pre, mod_ref, r, 0, d)
    ext_ref[POOL_HALO + tm:, :] = _prenorm(xn_ref[...], g_pre, mod_ref, r, 0, d) * keep_next

    pos = pos0 + lax.broadcasted_iota(jnp.int32, (tm, 1), 0)
    for g, w in enumerate(POOL_WINDOWS):
        cols = slice(g * cg, (g + 1) * cg)
        acc = ext_ref[POOL_HALO - w // 2:POOL_HALO - w // 2 + tm, cols]
        for dlt in range(-w // 2 + 1, w // 2):
            acc = acc + ext_ref[POOL_HALO + dlt:POOL_HALO + dlt + tm, cols]
        lo = jnp.maximum(pos - w // 2, 0)
        hi = jnp.minimum(pos - w // 2 + w, slen)
        mean = acc / (hi - lo).astype(F32)
        diff = (mean - ext_ref[POOL_HALO:POOL_HALO + tm, cols]).astype(BF16)
        y_ref[:, cols] = jnp.dot(diff, wp_ref[g], preferred_element_type=F32) * ps_ref[:, cols]

    gate = _mod_row(mod_ref, r, 2, d)
    o_ref[...] = x_ref[...] + gate * _rms(y_ref[...], gpost_ref[...], NORM_EPS)


def _pool_mix(xu, mod, gpre, gpost, wp, ps, *, layer, ic, seq, ctx_len, n_batch, rows):
    d = xu.shape[1]
    n_lat_rows = n_batch * seq
    tm = _tile(256, seq, *([ctx_len] if rows > n_lat_rows else []))
    assert tm % POOL_HALO == 0 and POOL_HALO >= max(POOL_WINDOWS) // 2
    hb = tm // POOL_HALO
    last_hblk = xu.shape[0] // POOL_HALO - 1
    kern = functools.partial(_pool_kernel, tm=tm, seq=seq, ctx_len=ctx_len, n_batch=n_batch,
                             n_lat_rows=n_lat_rows)
    return pl.pallas_call(
        kern,
        out_shape=jax.ShapeDtypeStruct((rows, d), F32),
        grid=(rows // tm,),
        in_specs=[
            pl.BlockSpec((tm, d), lambda i: (i, 0)),
            pl.BlockSpec((POOL_HALO, d), lambda i: (jnp.maximum(i * hb - 1, 0), 0)),
            pl.BlockSpec((POOL_HALO, d), lambda i: (jnp.minimum((i + 1) * hb, last_hblk), 0)),
            _mod_spec(mod, layer, 1),
            _vec_spec(d, layer),
            _vec_spec(d, layer),
            pl.BlockSpec((None,) + wp.shape[1:], lambda i: (ic, 0, 0, 0)),
            _vec_spec(d, ic),
        ],
        out_specs=pl.BlockSpec((tm, d), lambda i: (i, 0)),
        scratch_shapes=[pltpu.VMEM((tm + 2 * POOL_HALO, d), F32), pltpu.VMEM((tm, d), F32)],
        compiler_params=_cparams(1),
        name="pool_mix",
    )(xu, xu, xu, mod, gpre, gpost, wp, ps)


def kernel(x, c, ctx, c_ctx, w_mod, b_mod, g_mix_pre, g_mix_post, g_mlp_pre, g_mlp_post,
           w_qkv, w_attn_out, lambda_q1, lambda_k1, lambda_q2, lambda_k2, g_subln,
           w_fourier_out, w_pool, pool_scale, w_mlp_in, w_mlp_out):
    n_batch, seq, d = x.shape
    ctx_len = ctx.shape[1]
    depth = w_mod.shape[0]
    n_lat_rows = n_batch * seq
    n_ctx_rows = n_batch * ctx_len
    n_rows = n_lat_rows + n_ctx_rows
    n_heads = w_attn_out.shape[1] // V_HEAD_DIM

    n_cond = -(-(n_batch + 1) // 8) * 8
    cond = jnp.concatenate(
        [c, c_ctx[None, :], jnp.zeros((n_cond - n_batch - 1, d), F32)], axis=0)
    mod = _mod_all(cond, w_mod, b_mod)

    w_o_b = w_attn_out.astype(BF16)
    w_f_b = w_fourier_out.astype(BF16)
    w_p_b = w_pool.astype(BF16)
    w1_b = w_mlp_in[0].astype(BF16)
    w2_b = w_mlp_out[0].astype(BF16)

    row3 = lambda t: t.reshape(t.shape[0], 1, t.shape[1])
    g_mix_pre, g_mix_post, g_mlp_pre, g_mlp_post = map(
        row3, (g_mix_pre, g_mix_post, g_mlp_pre, g_mlp_post))
    lambda_q1, lambda_k1, lambda_q2, lambda_k2, g_subln, pool_scale = map(
        row3, (lambda_q1, lambda_k1, lambda_q2, lambda_k2, g_subln, pool_scale))
    lam_params = (lambda_q1, lambda_k1, lambda_q2, lambda_k2)

    rope = _rope_tables(seq)
    common = dict(seq=seq, n_batch=n_batch)

    x_l = x.reshape(n_lat_rows, d)
    x_c = ctx.reshape(n_ctx_rows, d)
    xu = None

    ia = ib = ic = 0
    for i in range(depth):
        last = i == depth - 1
        kind = i % N_MIXERS
        out_rows = n_lat_rows if last else n_rows
        if xu is None and kind != 0:
            xu = jnp.concatenate([x_l, x_c], axis=0)
        if kind == 0:
            lam_init = 0.8 - 0.6 * math.exp(-0.3 * i)
            qkw = dict(layer=i, ia=ia, **common)
            akw = dict(ia=ia, lam_init=lam_init, n_batch=n_batch)
            if xu is None:
                qkv_l = _qkv_proj(x_l, mod, g_mix_pre, w_qkv, rope, row0=0, **qkw)
                qkv_c = _qkv_proj(x_c, mod, g_mix_pre, w_qkv, rope, row0=n_lat_rows, **qkw)
                ctx_row0 = 0
            else:
                qkv_l = qkv_c = _qkv_proj(xu, mod, g_mix_pre, w_qkv, rope, row0=0, **qkw)
                ctx_row0 = n_lat_rows
            ctx_seg = (qkv_c, ctx_row0, ctx_len)
            o_l = _attention(qkv_l, 0, seq, [(qkv_l, 0, seq), ctx_seg], lam_params, g_subln,
                             hb=1, **akw)
            o_c = None if last else _attention(
                qkv_c, ctx_row0, ctx_len, [ctx_seg], lam_params, g_subln, hb=n_heads, **akw)
            if xu is None:
                xu = _proj_res(o_l, o_c, x_l, None if last else x_c, mod, g_mix_post, w_o_b,
                               layer=i, iw=ia, **common)
            else:
                xu = _proj_res(o_l, o_c, xu, None, mod, g_mix_post, w_o_b,
                               layer=i, iw=ia, **common)
            ia += 1
        elif kind == 1:
            cg = d // N_FFT_GROUPS
            cc, sc = _dft_tables(cg)
            cs = jnp.concatenate([cc, sc], axis=1).astype(BF16)
            a, b = _fourier_channels(xu, mod, g_mix_pre, cs, layer=i, rows=out_rows, **common)
            tc, ts = _dft_tables(seq)
            y_l = _fourier_seq(a, b, tc.astype(BF16), (-ts).astype(BF16),
                               n_batch=n_batch, length=seq, row0=0)
            y_c = None
            if not last:
                tcc, tsc = _dft_tables(ctx_len)
                y_c = _fourier_seq(a, b, tcc.astype(BF16), (-tsc).astype(BF16),
                                   n_batch=n_batch, length=ctx_len, row0=n_lat_rows)
            xu = _proj_res(y_l, y_c, xu, None, mod, g_mix_post, w_f_b, layer=i, iw=ib, **common)
            ib += 1
        else:
            xu = _pool_mix(xu, mod, g_mix_pre, g_mix_post, w_p_b, pool_scale, layer=i, ic=ic,
                           ctx_len=ctx_len, rows=out_rows, **common)
            ic += 1
        w_next = None if last else (w_mlp_in, w_mlp_out)
        xu, w1_b, w2_b = _mlp(xu, mod, g_mlp_pre, g_mlp_post, w1_b, w2_b, w_next, layer=i,
                              rows=out_rows, **common)
    return xu[:n_lat_rows].reshape(n_batch, seq, d)
```

```python
import functools
import math

import jax
import jax.numpy as jnp
from jax import lax
from jax.experimental import pallas as pl
from jax.experimental.pallas import tpu as pltpu

F32 = jnp.float32
BF16 = jnp.bfloat16

GRID_W = 64
N_MIXERS = 3
HEAD_DIM = 64
V_HEAD_DIM = 2 * HEAD_DIM
ROPE_BASE = 10000.0
AXIS_ROT = HEAD_DIM // 2
SUBLN_EPS = 1e-5
N_FFT_GROUPS = 4
POOL_WINDOWS = (2, 4, 8, 16)
NORM_EPS = 1e-6
N_MOD = 6
POOL_HALO = 8

LANES = 128
V7X_VMEM_BYTES = 64 * 1024 * 1024
VMEM_LIMIT = V7X_VMEM_BYTES - 4 * 1024 * 1024


def _cparams(n_axes):
    return pltpu.CompilerParams(
        dimension_semantics=("arbitrary",) * n_axes, vmem_limit_bytes=VMEM_LIMIT)


def _tile(pref, *counts):
    t = min(pref, *counts)
    while t > 8 and any(c % t for c in counts):
        t -= 8
    assert t >= 8 and all(c % t == 0 for c in counts), (pref, counts)
    return t


def _rms(x, g, eps):
    return x * lax.rsqrt(jnp.mean(x * x, axis=-1, keepdims=True) + eps) * g


def _mod_row(mod_ref, r, k, d):
    return mod_ref[pl.ds(r, 1), k * d:(k + 1) * d]


def _prenorm(x, g, mod_ref, r, k_shift, d):
    sh = _mod_row(mod_ref, r, k_shift, d)
    sc = _mod_row(mod_ref, r, k_shift + 1, d)
    inv = lax.rsqrt(jnp.mean(x * x, axis=-1, keepdims=True) + NORM_EPS)
    return (x * inv) * (g * (1.0 + sc)) + sh


def _mod_spec(mod, layer, n_axes):
    zeros = (0,) * 2
    return pl.BlockSpec((None,) + mod.shape[1:], lambda *_: (layer,) + zeros)


def _vec_spec(width, index):
    return pl.BlockSpec((None, 1, width), lambda *_: (index, 0, 0))


def _mod_kernel(cond_ref, w_ref, b_ref, o_ref):
    s = jax.nn.silu(cond_ref[...]).astype(BF16)
    o_ref[...] = jnp.dot(s, w_ref[...].astype(BF16), preferred_element_type=F32) + b_ref[...]


def _mod_all(cond, w_mod, b_mod):
    depth, d, n = w_mod.shape
    rows = cond.shape[0]
    tn = _tile(1024, n)
    return pl.pallas_call(
        _mod_kernel,
        out_shape=jax.ShapeDtypeStruct((depth, rows, n), F32),
        grid=(depth, n // tn),
        in_specs=[
            pl.BlockSpec((rows, d), lambda l, j: (0, 0)),
            pl.BlockSpec((None, d, tn), lambda l, j: (l, 0, j)),
            pl.BlockSpec((None, 1, tn), lambda l, j: (l, 0, j)),
        ],
        out_specs=pl.BlockSpec((None, rows, tn), lambda l, j: (l, 0, j)),
        compiler_params=_cparams(2),
        name="ada_mod",
    )(cond, w_mod, b_mod.reshape(depth, 1, n))


def _qkv_kernel(x_ref, mod_ref, g_ref, w_ref, tab_ref, o_ref, u_ref, *, tm, tn, seq, n_batch, row0):
    i = pl.program_id(0)
    d = x_ref.shape[1]

    @pl.when(pl.program_id(1) == 0)
    def _():
        r = jnp.minimum(lax.div(row0 + i * tm, seq), n_batch)
        u_ref[...] = _prenorm(x_ref[...], g_ref[...], mod_ref, r, 0, d).astype(BF16)

    rb = _tile(256, tm)
    w = w_ref[...].astype(BF16)
    for b in range(tm // rb):
        rows = slice(b * rb, (b + 1) * rb)
        y = jnp.dot(u_ref[rows, :], w, preferred_element_type=F32)
        cos = tab_ref[0, rows, :]
        sa = tab_ref[1, rows, :]
        sb = tab_ref[2, rows, :]
        for c in range(tn // LANES):
            yc = y[:, c * LANES:(c + 1) * LANES]
            rot = (yc * cos + pltpu.roll(yc, LANES - AXIS_ROT // 2, 1) * sa
                   + pltpu.roll(yc, AXIS_ROT // 2, 1) * sb)
            o_ref[c, rows, :] = rot.astype(BF16)


def _qkv_proj(xa, mod, g, w, rope, *, layer, ia, seq, n_batch, row0):
    rows, d = xa.shape
    n = w.shape[2]
    n_lat_rows = n_batch * seq
    n_lat_here = max(0, min(rows, n_lat_rows - row0))
    tm = _tile(1024, seq, *[c for c in (n_lat_here, rows - n_lat_here) if c])
    tn = _tile(512, n // 3)
    assert tn % LANES == 0 and row0 % tm == 0
    lat_tiles_per_seq = seq // tm
    n_lat_tiles = n_lat_here // tm
    n_q_j = n // 3 // tn

    def tab_index(i, j):
        lat = jnp.where(j < n_q_j, 0, jnp.where(j < 2 * n_q_j, 1, 2))
        ctx = jnp.where(j < n_q_j, 3, 2)
        return (jnp.where(i < n_lat_tiles, lat, ctx), 0, (row0 // tm + i) % lat_tiles_per_seq, 0)

    kern = functools.partial(_qkv_kernel, tm=tm, tn=tn, seq=seq, n_batch=n_batch, row0=row0)
    return pl.pallas_call(
        kern,
        out_shape=jax.ShapeDtypeStruct((n // LANES, rows, LANES), BF16),
        grid=(rows // tm, n // tn),
        in_specs=[
            pl.BlockSpec((tm, d), lambda i, j: (i, 0)),
            _mod_spec(mod, layer, 2),
            _vec_spec(d, layer),
            pl.BlockSpec((None, d, tn), lambda i, j: (ia, 0, j)),
            pl.BlockSpec((None, 3, tm, LANES), tab_index),
        ],
        out_specs=pl.BlockSpec((tn // LANES, tm, LANES), lambda i, j: (j, i, 0)),
        scratch_shapes=[pltpu.VMEM((tm, d), BF16)],
        compiler_params=_cparams(2),
        name="qkv_proj",
    )(xa, mod, g, w, rope)


def _rope_tables(seq):
    rows = seq // GRID_W
    row = jnp.repeat(jnp.arange(rows), GRID_W).astype(F32)
    col = jnp.tile(jnp.arange(GRID_W), rows).astype(F32)
    n_freq = AXIS_ROT // 2
    inv = 1.0 / (ROPE_BASE ** (jnp.arange(n_freq, dtype=F32) / n_freq))
    ang_r = row[:, None] * inv
    ang_c = col[:, None] * inv
    cr, sr, cc, sc = jnp.cos(ang_r), jnp.sin(ang_r), jnp.cos(ang_c), jnp.sin(ang_c)
    z = jnp.zeros_like(sr)
    reps = LANES // HEAD_DIM
    cos = jnp.tile(jnp.concatenate([cr, cr, cc, cc], axis=1), (1, reps))
    sa = jnp.tile(jnp.concatenate([-sr, z, -sc, z], axis=1), (1, reps))
    sb = jnp.tile(jnp.concatenate([z, sr, z, sc], axis=1), (1, reps))
    rot = jnp.stack([cos, sa, sb])
    ident = jnp.stack([jnp.ones_like(cos), jnp.zeros_like(cos), jnp.zeros_like(cos)])
    qs = math.log2(math.e) / math.sqrt(HEAD_DIM)
    return jnp.stack([rot * qs, rot, ident, ident * qs])


def _lambda(lq1_ref, lk1_ref, lq2_ref, lk2_ref, lam_init):
    return (jnp.exp(jnp.sum(lq1_ref[...] * lk1_ref[...], axis=-1, keepdims=True))
            - jnp.exp(jnp.sum(lq2_ref[...] * lk2_ref[...], axis=-1, keepdims=True)) + lam_init)


def _attn_kernel(*refs, n_seg, lam_init, rb):
    q_ref = refs[0]
    seg_refs = refs[1:1 + 2 * n_seg]
    lq1_ref, lk1_ref, lq2_ref, lk2_ref, gs_ref, o_ref, kk_ref, va_ref = refs[1 + 2 * n_seg:]
    hb, q_len, _ = q_ref.shape
    n_blocks = q_len // rb
    lam = _lambda(lq1_ref, lk1_ref, lq2_ref, lk2_ref, lam_init)
    gs = gs_ref[...]
    lane = lax.broadcasted_iota(jnp.int32, (rb, LANES), 1)
    va_ref[:, V_HEAD_DIM:] = jnp.ones((va_ref.shape[0], V_HEAD_DIM), BF16)

    for h in range(hb):
        off = 0
        for s in range(n_seg):
            n = seg_refs[2 * s].shape[1]
            kk_ref[off:off + n, :] = seg_refs[2 * s][h]
            va_ref[off:off + n, 0:V_HEAD_DIM] = seg_refs[2 * s + 1][h]
            off += n

        def scores(r):
            q = q_ref[h, r * rb:(r + 1) * rb, :]
            qq = jnp.concatenate([jnp.where(lane < HEAD_DIM, q, jnp.zeros_like(q)),
                                  jnp.where(lane >= HEAD_DIM, q, jnp.zeros_like(q))], axis=0)
            return lax.dot_general(qq, kk_ref[...], (((1,), (1,)), ((), ())),
                                   preferred_element_type=F32)

        def finish(r, s):
            p = jnp.exp2(s - jnp.max(s, axis=-1, keepdims=True)).astype(BF16)
            acc = jnp.dot(p, va_ref[...], preferred_element_type=F32)
            ratio = acc[:, :V_HEAD_DIM] / acc[:, V_HEAD_DIM:]
            o = ratio[:rb] - lam * ratio[rb:]
            o = _rms(o, gs, SUBLN_EPS) * (1.0 - lam_init)
            o_ref[r * rb:(r + 1) * rb, h * V_HEAD_DIM:(h + 1) * V_HEAD_DIM] = o.astype(BF16)

        s_next = scores(0)
        for r in range(n_blocks):
            s_cur = s_next
            if r + 1 < n_blocks:
                s_next = scores(r + 1)
            finish(r, s_cur)


def _attention(q_arr, q_row0, q_len, segs, lam_params, gs, *, ia, lam_init, n_batch, hb):
    n_heads = q_arr.shape[0] // 3
    assert n_heads % hb == 0 and q_row0 % q_len == 0
    nk = sum(n for _, _, n in segs)
    q_blk0 = q_row0 // q_len
    in_specs = [pl.BlockSpec((hb, q_len, LANES), lambda b, h: (h, q_blk0 + b, 0))]
    args = [q_arr]
    for arr, row0, n in segs:
        assert row0 % n == 0 and arr.shape[0] == 3 * n_heads
        for part in (1, 2):
            in_specs.append(pl.BlockSpec(
                (hb, n, LANES),
                lambda b, h, blk0=row0 // n, col0=part * n_heads // hb: (col0 + h, blk0 + b, 0)))
            args.append(arr)
    in_specs += [_vec_spec(HEAD_DIM, ia)] * 4
    in_specs.append(_vec_spec(V_HEAD_DIM, ia))
    return pl.pallas_call(
        functools.partial(_attn_kernel, n_seg=len(segs), lam_init=lam_init,
                          rb=_tile(128, q_len)),
        out_shape=jax.ShapeDtypeStruct((n_batch * q_len, n_heads * V_HEAD_DIM), BF16),
        grid=(n_batch, n_heads // hb),
        in_specs=in_specs,
        out_specs=pl.BlockSpec((q_len, hb * V_HEAD_DIM), lambda b, h: (b, h)),
        scratch_shapes=[pltpu.VMEM((nk, LANES), BF16), pltpu.VMEM((nk, 2 * V_HEAD_DIM), BF16)],
        compiler_params=_cparams(2),
        name="diff_attn",
    )(*args, *lam_params, gs)


def _proj_res_kernel(*refs, tm, seq, n_batch, n_lat_tiles, has_ctx, split_x):
    refs = list(refs)
    inl_ref = refs.pop(0)
    inc_ref = refs.pop(0) if has_ctx else None
    xl_ref = refs.pop(0)
    xc_ref = refs.pop(0) if split_x else xl_ref
    mod_ref, g_ref, w_ref, o_ref = refs
    i = pl.program_id(0)
    d = o_ref.shape[1]

    def body(inp_ref, x_ref):
        y = jnp.dot(inp_ref[...], w_ref[...], preferred_element_type=F32)
        r = jnp.minimum(lax.div(i * tm, seq), n_batch)
        gate = _mod_row(mod_ref, r, 2, d)
        o_ref[...] = x_ref[...] + gate * _rms(y, g_ref[...], NORM_EPS)

    if has_ctx:
        @pl.when(i < n_lat_tiles)
        def _():
            body(inl_ref, xl_ref)

        @pl.when(i >= n_lat_tiles)
        def _():
            body(inc_ref, xc_ref)
    else:
        body(inl_ref, xl_ref)


def _proj_res(inp_l, inp_c, x_l, x_c, mod, g, w, *, layer, iw, seq, n_batch):
    d = x_l.shape[1]
    n_lat_rows = inp_l.shape[0]
    has_ctx = inp_c is not None
    split_x = x_c is not None
    assert has_ctx or not split_x
    rows = n_lat_rows + (inp_c.shape[0] if has_ctx else 0)
    tm = _tile(512, seq, *([inp_c.shape[0]] if has_ctx else []))
    n_lat_tiles = n_lat_rows // tm
    kin = inp_l.shape[1]
    lat_index = lambda i: (jnp.minimum(i, n_lat_tiles - 1), 0)
    ctx_index = lambda i: (jnp.maximum(i - n_lat_tiles, 0), 0)
    in_specs = [pl.BlockSpec((tm, kin), lat_index)]
    args = [inp_l]
    if has_ctx:
        in_specs.append(pl.BlockSpec((tm, kin), ctx_index))
        args.append(inp_c)
    if split_x:
        in_specs += [pl.BlockSpec((tm, d), lat_index), pl.BlockSpec((tm, d), ctx_index)]
        args += [x_l, x_c]
    else:
        in_specs.append(pl.BlockSpec((tm, d), lambda i: (i, 0)))
        args.append(x_l)
    in_specs += [
        _mod_spec(mod, layer, 1),
        _vec_spec(d, layer),
        pl.BlockSpec((None,) + w.shape[1:], lambda i: (iw, 0, 0)),
    ]
    args += [mod, g, w]
    kern = functools.partial(_proj_res_kernel, tm=tm, seq=seq, n_batch=n_batch,
                             n_lat_tiles=n_lat_tiles, has_ctx=has_ctx, split_x=split_x)
    return pl.pallas_call(
        kern,
        out_shape=jax.ShapeDtypeStruct((rows, d), F32),
        grid=(rows // tm,),
        in_specs=in_specs,
        out_specs=pl.BlockSpec((tm, d), lambda i: (i, 0)),
        compiler_params=_cparams(1),
        name="proj_res",
    )(*args)


def _mlp_kernel(*refs, tm, rb, seq, n_batch, cast_next):
    if cast_next:
        (x_ref, mod_ref, gpre_ref, gpost_ref, w1_ref, w2_ref, w1n_ref, w2n_ref,
         o_ref, w1o_ref, w2o_ref, u_ref) = refs
    else:
        x_ref, mod_ref, gpre_ref, gpost_ref, w1_ref, w2_ref, o_ref, u_ref = refs
    i = pl.program_id(0)
    j = pl.program_id(1)
    d = x_ref.shape[1]
    r = jnp.minimum(lax.div(i * tm, seq), n_batch)

    @pl.when(j == 0)
    def _():
        u_ref[...] = _prenorm(x_ref[...], gpre_ref[...], mod_ref, r, 3, d).astype(BF16)

    def hidden(b):
        h = jnp.dot(u_ref[b * rb:(b + 1) * rb, :], w1_ref[...], preferred_element_type=F32)
        h = jnp.maximum(h, 0.0)
        return (h * h).astype(BF16)

    def chunk(first):
        n_blocks = tm // rb
        h_next = hidden(0)
        for b in range(n_blocks):
            h_cur = h_next
            if b + 1 < n_blocks:
                h_next = hidden(b + 1)
            rows = slice(b * rb, (b + 1) * rb)
            part = jnp.dot(h_cur, w2_ref[...], preferred_element_type=F32)
            o_ref[rows, :] = part if first else o_ref[rows, :] + part
        if cast_next:
            w1o_ref[...] = w1n_ref[...].astype(BF16)
            w2o_ref[...] = w2n_ref[...].astype(BF16)

    @pl.when(j == 0)
    def _():
        chunk(True)

    @pl.when(j > 0)
    def _():
        chunk(False)

    @pl.when(j == pl.num_programs(1) - 1)
    def _():
        y = o_ref[...]
        inv = lax.rsqrt(jnp.mean(y * y, axis=-1, keepdims=True) + NORM_EPS)
        gg = gpost_ref[...] * _mod_row(mod_ref, r, 5, d)
        o_ref[...] = x_ref[...] + (y * inv) * gg


CAST_BLOCK_BYTES = 1024 * 1024


def _mlp(xu, mod, gpre, gpost, w1, w2, w_next, *, layer, seq, n_batch, rows):
    d = xu.shape[1]
    dff = w1.shape[1]
    tm = _tile(1024, seq, *([xu.shape[0] - n_batch * seq] if rows > n_batch * seq else []))
    tf = _tile(512, dff)
    n_j = dff // tf
    n_steps = (rows // tm) * n_j
    cast_next = w_next is not None
    kern = functools.partial(_mlp_kernel, tm=tm, rb=_tile(512, tm), seq=seq, n_batch=n_batch,
                             cast_next=cast_next)
    in_specs = [
        pl.BlockSpec((tm, d), lambda i, j: (i, 0)),
        _mod_spec(mod, layer, 2),
        _vec_spec(d, layer),
        _vec_spec(d, layer),
        pl.BlockSpec((d, tf), lambda i, j: (0, j)),
        pl.BlockSpec((tf, d), lambda i, j: (j, 0)),
    ]
    args = [xu, mod, gpre, gpost, w1, w2]
    out_shape = [jax.ShapeDtypeStruct((rows, d), F32)]
    out_specs = [pl.BlockSpec((tm, d), lambda i, j: (i, 0))]
    if cast_next:
        r1 = _tile(CAST_BLOCK_BYTES // (4 * dff), d)
        r2 = _tile(CAST_BLOCK_BYTES // (4 * d), dff)
        c1, c2 = d // r1, dff // r2
        assert c1 + c2 <= n_steps

        def blk1(i, j):
            return jnp.minimum(i * n_j + j, c1 - 1)

        def blk2(i, j):
            return jnp.clip(i * n_j + j - c1, 0, c2 - 1)

        in_specs += [pl.BlockSpec((None, r1, dff), lambda i, j: (layer + 1, blk1(i, j), 0)),
                     pl.BlockSpec((None, r2, d), lambda i, j: (layer + 1, blk2(i, j), 0))]
        args += list(w_next)
        out_shape += [jax.ShapeDtypeStruct((d, dff), BF16), jax.ShapeDtypeStruct((dff, d), BF16)]
        out_specs += [pl.BlockSpec((r1, dff), lambda i, j: (blk1(i, j), 0)),
                      pl.BlockSpec((r2, d), lambda i, j: (blk2(i, j), 0))]
    outs = pl.pallas_call(
        kern,
        out_shape=out_shape,
        grid=(rows // tm, n_j),
        in_specs=in_specs,
        out_specs=out_specs,
        scratch_shapes=[pltpu.VMEM((tm, d), BF16)],
        compiler_params=_cparams(2),
        name="sq_relu_mlp",
    )(*args)
    return outs if cast_next else (outs[0], None, None)


def _dft_tables(n):
    b = max(f for f in range(1, math.isqrt(n) + 1) if n % f == 0)
    a = n // b
    j = jnp.arange(n, dtype=jnp.int32)[:, None]
    ang1 = ((j * jnp.arange(a, dtype=jnp.int32)[None, :]) % a).astype(F32) * (2.0 * math.pi / a)
    ang0 = ((j * jnp.arange(b, dtype=jnp.int32)[None, :]) % n).astype(F32) * (2.0 * math.pi / n)
    c1, s1 = jnp.cos(ang1)[:, :, None], jnp.sin(ang1)[:, :, None]
    c0, s0 = jnp.cos(ang0)[:, None, :], jnp.sin(ang0)[:, None, :]
    scale = 1.0 / math.sqrt(n)
    cos = (c1 * c0 - s1 * s0).reshape(n, n) * scale
    sin = (s1 * c0 + c1 * s0).reshape(n, n) * scale
    return cos, sin


def _fourier_ch_kernel(x_ref, mod_ref, g_ref, cs_ref, a_ref, b_ref, *, tm, seq, n_batch):
    i = pl.program_id(0)
    d = x_ref.shape[1]
    cg = d // N_FFT_GROUPS
    r = jnp.minimum(lax.div(i * tm, seq), n_batch)
    u = _prenorm(x_ref[...], g_ref[...], mod_ref, r, 0, d).astype(BF16)
    for g in range(N_FFT_GROUPS):
        ab = jnp.dot(u[:, g * cg:(g + 1) * cg], cs_ref[...], preferred_element_type=F32)
        a_ref[:, g * cg:(g + 1) * cg] = ab[:, :cg].astype(BF16)
        b_ref[:, g * cg:(g + 1) * cg] = ab[:, cg:].astype(BF16)


def _fourier_channels(xu, mod, g, cs, *, layer, seq, n_batch, rows):
    d = xu.shape[1]
    tm = _tile(512, seq, *([xu.shape[0] - n_batch * seq] if rows > n_batch * seq else []))
    kern = functools.partial(_fourier_ch_kernel, tm=tm, seq=seq, n_batch=n_batch)
    return pl.pallas_call(
        kern,
        out_shape=[jax.ShapeDtypeStruct((rows, d), BF16)] * 2,
        grid=(rows // tm,),
        in_specs=[
            pl.BlockSpec((tm, d), lambda i: (i, 0)),
            _mod_spec(mod, layer, 1),
            _vec_spec(d, layer),
            pl.BlockSpec(cs.shape, lambda i: (0, 0)),
        ],
        out_specs=[pl.BlockSpec((tm, d), lambda i: (i, 0))] * 2,
        compiler_params=_cparams(1),
        name="fourier_channels",
    )(xu, mod, g, cs)


def _fourier_seq_kernel(tc_ref, tms_ref, a_ref, b_ref, o_ref, *, tmm):
    n = tc_ref.shape[0]
    for mi in range(n // tmm):
        rows = slice(mi * tmm, (mi + 1) * tmm)
        y = (jnp.dot(tc_ref[rows, :], a_ref[...], preferred_element_type=F32)
             + jnp.dot(tms_ref[rows, :], b_ref[...], preferred_element_type=F32))
        o_ref[rows, :] = y.astype(BF16)


def _fourier_seq(a, b, tc, tms, *, n_batch, length, row0):
    d = a.shape[1]
    tn = _tile(512, d)
    tmm = _tile(512, length)
    blk0 = row0 // length
    assert row0 % length == 0
    ab_spec = pl.BlockSpec((length, tn), lambda bi, ni: (blk0 + bi, ni))
    tab_spec = pl.BlockSpec((length, length), lambda bi, ni: (0, 0))
    return pl.pallas_call(
        functools.partial(_fourier_seq_kernel, tmm=tmm),
        out_shape=jax.ShapeDtypeStruct((n_batch * length, d), BF16),
        grid=(n_batch, d // tn),
        in_specs=[tab_spec, tab_spec, ab_spec, ab_spec],
        out_specs=pl.BlockSpec((length, tn), lambda bi, ni: (bi, ni)),
        compiler_params=_cparams(2),
        name="fourier_seq",
    )(tc, tms, a, b)


def _pool_kernel(x_ref, xp_ref, xn_ref, mod_ref, gpre_ref, gpost_ref, wp_ref, ps_ref, o_ref,
                 ext_ref, y_ref, *, tm, seq, ctx_len, n_batch, n_lat_rows):
    i = pl.program_id(0)
    d = x_ref.shape[1]
    cg = d // len(POOL_WINDOWS)
    row0 = i * tm
    is_lat = row0 < n_lat_rows
    slen = jnp.where(is_lat, seq, ctx_len)
    pos0 = jnp.where(is_lat, lax.rem(row0, seq), lax.rem(row0 - n_lat_rows, ctx_len))
    r = jnp.minimum(lax.div(row0, seq), n_batch)
    g_pre = gpre_ref[...]
    keep_prev = jnp.where(pos0 > 0, 1.0, 0.0).astype(F32)
    keep_next = jnp.where(pos0 + tm < slen, 1.0, 0.0).astype(F32)
    ext_ref[0:POOL_HALO, :] = _prenorm(xp_ref[...], g_pre, mod_ref, r, 0, d) * keep_prev
    ext_ref[POOL_HALO:POOL_HALO + tm, :] = _prenorm(x_ref[...], g_pre, mod_ref, r, 0, d)
    ext_ref[POOL_HALO + tm:, :] = _prenorm(xn_ref[...], g_pre, mod_ref, r, 0, d) * keep_next

    pos = pos0 + lax.broadcasted_iota(jnp.int32, (tm, 1), 0)
    for g, w in enumerate(POOL_WINDOWS):
        cols = slice(g * cg, (g + 1) * cg)
        acc = ext_ref[POOL_HALO - w // 2:POOL_HALO - w // 2 + tm, cols]
        for dlt in range(-w // 2 + 1, w // 2):
            acc = acc + ext_ref[POOL_HALO + dlt:POOL_HALO + dlt + tm, cols]
        lo = jnp.maximum(pos - w // 2, 0)
        hi = jnp.minimum(pos - w // 2 + w, slen)
        mean = acc / (hi - lo).astype(F32)
        diff = (mean - ext_ref[POOL_HALO:POOL_HALO + tm, cols]).astype(BF16)
        y_ref[:, cols] = jnp.dot(diff, wp_ref[g], preferred_element_type=F32) * ps_ref[:, cols]

    gate = _mod_row(mod_ref, r, 2, d)
    o_ref[...] = x_ref[...] + gate * _rms(y_ref[...], gpost_ref[...], NORM_EPS)


def _pool_mix(xu, mod, gpre, gpost, wp, ps, *, layer, ic, seq, ctx_len, n_batch, rows):
    d = xu.shape[1]
    n_lat_rows = n_batch * seq
    tm = _tile(256, seq, *([ctx_len] if rows > n_lat_rows else []))
    assert tm % POOL_HALO == 0 and POOL_HALO >= max(POOL_WINDOWS) // 2
    hb = tm // POOL_HALO
    last_hblk = xu.shape[0] // POOL_HALO - 1
    kern = functools.partial(_pool_kernel, tm=tm, seq=seq, ctx_len=ctx_len, n_batch=n_batch,
                             n_lat_rows=n_lat_rows)
    return pl.pallas_call(
        kern,
        out_shape=jax.ShapeDtypeStruct((rows, d), F32),
        grid=(rows // tm,),
        in_specs=[
            pl.BlockSpec((tm, d), lambda i: (i, 0)),
            pl.BlockSpec((POOL_HALO, d), lambda i: (jnp.maximum(i * hb - 1, 0), 0)),
            pl.BlockSpec((POOL_HALO, d), lambda i: (jnp.minimum((i + 1) * hb, last_hblk), 0)),
            _mod_spec(mod, layer, 1),
            _vec_spec(d, layer),
            _vec_spec(d, layer),
            pl.BlockSpec((None,) + wp.shape[1:], lambda i: (ic, 0, 0, 0)),
            _vec_spec(d, ic),
        ],
        out_specs=pl.BlockSpec((tm, d), lambda i: (i, 0)),
        scratch_shapes=[pltpu.VMEM((tm + 2 * POOL_HALO, d), F32), pltpu.VMEM((tm, d), F32)],
        compiler_params=_cparams(1),
        name="pool_mix",
    )(xu, xu, xu, mod, gpre, gpost, wp, ps)


def kernel(x, c, ctx, c_ctx, w_mod, b_mod, g_mix_pre, g_mix_post, g_mlp_pre, g_mlp_post,
           w_qkv, w_attn_out, lambda_q1, lambda_k1, lambda_q2, lambda_k2, g_subln,
           w_fourier_out, w_pool, pool_scale, w_mlp_in, w_mlp_out):
    n_batch, seq, d = x.shape
    ctx_len = ctx.shape[1]
    depth = w_mod.shape[0]
    n_lat_rows = n_batch * seq
    n_ctx_rows = n_batch * ctx_len
    n_rows = n_lat_rows + n_ctx_rows
    n_heads = w_attn_out.shape[1] // V_HEAD_DIM

    n_cond = -(-(n_batch + 1) // 8) * 8
    cond = jnp.concatenate(
        [c, c_ctx[None, :], jnp.zeros((n_cond - n_batch - 1, d), F32)], axis=0)
    mod = _mod_all(cond, w_mod, b_mod)

    w_o_b = w_attn_out.astype(BF16)
    w_f_b = w_fourier_out.astype(BF16)
    w_p_b = w_pool.astype(BF16)
    w1_b = w_mlp_in[0].astype(BF16)
    w2_b = w_mlp_out[0].astype(BF16)

    row3 = lambda t: t.reshape(t.shape[0], 1, t.shape[1])
    g_mix_pre, g_mix_post, g_mlp_pre, g_mlp_post = map(
        row3, (g_mix_pre, g_mix_post, g_mlp_pre, g_mlp_post))
    lambda_q1, lambda_k1, lambda_q2, lambda_k2, g_subln, pool_scale = map(
        row3, (lambda_q1, lambda_k1, lambda_q2, lambda_k2, g_subln, pool_scale))
    lam_params = (lambda_q1, lambda_k1, lambda_q2, lambda_k2)

    rope = _rope_tables(seq)
    common = dict(seq=seq, n_batch=n_batch)

    x_l = x.reshape(n_lat_rows, d)
    x_c = ctx.reshape(n_ctx_rows, d)
    xu = None

    ia = ib = ic = 0
    for i in range(depth):
        last = i == depth - 1
        kind = i % N_MIXERS
        out_rows = n_lat_rows if last else n_rows
        if xu is None and kind != 0:
            xu = jnp.concatenate([x_l, x_c], axis=0)
        if kind == 0:
            lam_init = 0.8 - 0.6 * math.exp(-0.3 * i)
            qkw = dict(layer=i, ia=ia, **common)
            akw = dict(ia=ia, lam_init=lam_init, n_batch=n_batch)
            if xu is None:
                qkv_l = _qkv_proj(x_l, mod, g_mix_pre, w_qkv, rope, row0=0, **qkw)
                qkv_c = _qkv_proj(x_c, mod, g_mix_pre, w_qkv, rope, row0=n_lat_rows, **qkw)
                ctx_row0 = 0
            else:
                qkv_l = qkv_c = _qkv_proj(xu, mod, g_mix_pre, w_qkv, rope, row0=0, **qkw)
                ctx_row0 = n_lat_rows
            ctx_seg = (qkv_c, ctx_row0, ctx_len)
            o_l = _attention(qkv_l, 0, seq, [(qkv_l, 0, seq), ctx_seg], lam_params, g_subln,
                             hb=1, **akw)
            o_c = None if last else _attention(
                qkv_c, ctx_row0, ctx_len, [ctx_seg], lam_params, g_subln, hb=n_heads, **akw)
            if xu is None:
                xu = _proj_res(o_l, o_c, x_l, None if last else x_c, mod, g_mix_post, w_o_b,
                               layer=i, iw=ia, **common)
            else:
                xu = _proj_res(o_l, o_c, xu, None, mod, g_mix_post, w_o_b,
                               layer=i, iw=ia, **common)
            ia += 1
        elif kind == 1:
            cg = d // N_FFT_GROUPS
            cc, sc = _dft_tables(cg)
            cs = jnp.concatenate([cc, sc], axis=1).astype(BF16)
            a, b = _fourier_channels(xu, mod, g_mix_pre, cs, layer=i, rows=out_rows, **common)
            tc, ts = _dft_tables(seq)
            y_l = _fourier_seq(a, b, tc.astype(BF16), (-ts).astype(BF16),
                               n_batch=n_batch, length=seq, row0=0)
            y_c = None
            if not last:
                tcc, tsc = _dft_tables(ctx_len)
                y_c = _fourier_seq(a, b, tcc.astype(BF16), (-tsc).astype(BF16),
                                   n_batch=n_batch, length=ctx_len, row0=n_lat_rows)
            xu = _proj_res(y_l, y_c, xu, None, mod, g_mix_post, w_f_b, layer=i, iw=ib, **common)
            ib += 1
        else:
            xu = _pool_mix(xu, mod, g_mix_pre, g_mix_post, w_p_b, pool_scale, layer=i, ic=ic,
                           ctx_len=ctx_len, rows=out_rows, **common)
            ic += 1
        w_next = None if last else (w_mlp_in, w_mlp_out)
        xu, w1_b, w2_b = _mlp(xu, mod, g_mlp_pre, g_mlp_post, w1_b, w2_b, w_next, layer=i,
                              rows=out_rows, **common)
    return xu[:n_lat_rows].reshape(n_batch, seq, d)
```

```python
import functools
import math

import jax
import jax.numpy as jnp
from jax import lax
from jax.experimental import pallas as pl
from jax.experimental.pallas import tpu as pltpu

F32 = jnp.float32
BF16 = jnp.bfloat16

GRID_W = 64
N_MIXERS = 3
HEAD_DIM = 64
V_HEAD_DIM = 2 * HEAD_DIM
ROPE_BASE = 10000.0
AXIS_ROT = HEAD_DIM // 2
SUBLN_EPS = 1e-5
N_FFT_GROUPS = 4
POOL_WINDOWS = (2, 4, 8, 16)
NORM_EPS = 1e-6
N_MOD = 6
POOL_HALO = 8

LANES = 128
V7X_VMEM_BYTES = 64 * 1024 * 1024
VMEM_LIMIT = V7X_VMEM_BYTES - 4 * 1024 * 1024


def _cparams(n_axes):
    return pltpu.CompilerParams(
        dimension_semantics=("arbitrary",) * n_axes, vmem_limit_bytes=VMEM_LIMIT)


def _tile(pref, *counts):
    t = min(pref, *counts)
    while t > 8 and any(c % t for c in counts):
        t -= 8
    assert t >= 8 and all(c % t == 0 for c in counts), (pref, counts)
    return t


def _rms(x, g, eps):
    return x * lax.rsqrt(jnp.mean(x * x, axis=-1, keepdims=True) + eps) * g


def _mod_row(mod_ref, r, k, d):
    return mod_ref[pl.ds(r, 1), k * d:(k + 1) * d]


def _prenorm(x, g, mod_ref, r, k_shift, d):
    sh = _mod_row(mod_ref, r, k_shift, d)
    sc = _mod_row(mod_ref, r, k_shift + 1, d)
    inv = lax.rsqrt(jnp.mean(x * x, axis=-1, keepdims=True) + NORM_EPS)
    return (x * inv) * (g * (1.0 + sc)) + sh


def _mod_spec(mod, layer, n_axes):
    zeros = (0,) * 2
    return pl.BlockSpec((None,) + mod.shape[1:], lambda *_: (layer,) + zeros)


def _vec_spec(width, index):
    return pl.BlockSpec((None, 1, width), lambda *_: (index, 0, 0))


def _mod_kernel(cond_ref, w_ref, b_ref, o_ref):
    s = jax.nn.silu(cond_ref[...]).astype(BF16)
    o_ref[...] = jnp.dot(s, w_ref[...].astype(BF16), preferred_element_type=F32) + b_ref[...]


def _mod_all(cond, w_mod, b_mod):
    depth, d, n = w_mod.shape
    rows = cond.shape[0]
    tn = _tile(1024, n)
    return pl.pallas_call(
        _mod_kernel,
        out_shape=jax.ShapeDtypeStruct((depth, rows, n), F32),
        grid=(depth, n // tn),
        in_specs=[
            pl.BlockSpec((rows, d), lambda l, j: (0, 0)),
            pl.BlockSpec((None, d, tn), lambda l, j: (l, 0, j)),
            pl.BlockSpec((None, 1, tn), lambda l, j: (l, 0, j)),
        ],
        out_specs=pl.BlockSpec((None, rows, tn), lambda l, j: (l, 0, j)),
        compiler_params=_cparams(2),
        name="ada_mod",
    )(cond, w_mod, b_mod.reshape(depth, 1, n))


def _qkv_kernel(x_ref, mod_ref, g_ref, w_ref, tab_ref, o_ref, u_ref, *, tm, tn, seq, n_batch, row0):
    i = pl.program_id(0)
    d = x_ref.shape[1]

    @pl.when(pl.program_id(1) == 0)
    def _():
        r = jnp.minimum(lax.div(row0 + i * tm, seq), n_batch)
        u_ref[...] = _prenorm(x_ref[...], g_ref[...], mod_ref, r, 0, d).astype(BF16)

    rb = _tile(256, tm)
    w = w_ref[...].astype(BF16)
    for b in range(tm // rb):
        rows = slice(b * rb, (b + 1) * rb)
        y = jnp.dot(u_ref[rows, :], w, preferred_element_type=F32)
        cos = tab_ref[0, rows, :]
        sa = tab_ref[1, rows, :]
        sb = tab_ref[2, rows, :]
        for c in range(tn // LANES):
            yc = y[:, c * LANES:(c + 1) * LANES]
            rot = (yc * cos + pltpu.roll(yc, LANES - AXIS_ROT // 2, 1) * sa
                   + pltpu.roll(yc, AXIS_ROT // 2, 1) * sb)
            o_ref[c, rows, :] = rot.astype(BF16)


def _qkv_proj(xa, mod, g, w, rope, *, layer, ia, seq, n_batch, row0):
    rows, d = xa.shape
    n = w.shape[2]
    n_lat_rows = n_batch * seq
    n_lat_here = max(0, min(rows, n_lat_rows - row0))
    tm = _tile(1024, seq, *[c for c in (n_lat_here, rows - n_lat_here) if c])
    tn = _tile(512, n // 3)
    assert tn % LANES == 0 and row0 % tm == 0
    lat_tiles_per_seq = seq // tm
    n_lat_tiles = n_lat_here // tm
    n_q_j = n // 3 // tn

    def tab_index(i, j):
        lat = jnp.where(j < n_q_j, 0, jnp.where(j < 2 * n_q_j, 1, 2))
        ctx = jnp.where(j < n_q_j, 3, 2)
        return (jnp.where(i < n_lat_tiles, lat, ctx), 0, (row0 // tm + i) % lat_tiles_per_seq, 0)

    kern = functools.partial(_qkv_kernel, tm=tm, tn=tn, seq=seq, n_batch=n_batch, row0=row0)
    return pl.pallas_call(
        kern,
        out_shape=jax.ShapeDtypeStruct((n // LANES, rows, LANES), BF16),
        grid=(rows // tm, n // tn),
        in_specs=[
            pl.BlockSpec((tm, d), lambda i, j: (i, 0)),
            _mod_spec(mod, layer, 2),
            _vec_spec(d, layer),
            pl.BlockSpec((None, d, tn), lambda i, j: (ia, 0, j)),
            pl.BlockSpec((None, 3, tm, LANES), tab_index),
        ],
        out_specs=pl.BlockSpec((tn // LANES, tm, LANES), lambda i, j: (j, i, 0)),
        scratch_shapes=[pltpu.VMEM((tm, d), BF16)],
        compiler_params=_cparams(2),
        name="qkv_proj",
    )(xa, mod, g, w, rope)


def _rope_tables(seq):
    rows = seq // GRID_W
    row = jnp.repeat(jnp.arange(rows), GRID_W).astype(F32)
    col = jnp.tile(jnp.arange(GRID_W), rows).astype(F32)
    n_freq = AXIS_ROT // 2
    inv = 1.0 / (ROPE_BASE ** (jnp.arange(n_freq, dtype=F32) / n_freq))
    ang_r = row[:, None] * inv
    ang_c = col[:, None] * inv
    cr, sr, cc, sc = jnp.cos(ang_r), jnp.sin(ang_r), jnp.cos(ang_c), jnp.sin(ang_c)
    z = jnp.zeros_like(sr)
    reps = LANES // HEAD_DIM
    cos = jnp.tile(jnp.concatenate([cr, cr, cc, cc], axis=1), (1, reps))
    sa = jnp.tile(jnp.concatenate([-sr, z, -sc, z], axis=1), (1, reps))
    sb = jnp.tile(jnp.concatenate([z, sr, z, sc], axis=1), (1, reps))
    rot = jnp.stack([cos, sa, sb])
    ident = jnp.stack([jnp.ones_like(cos), jnp.zeros_like(cos), jnp.zeros_like(cos)])
    qs = math.log2(math.e) / math.sqrt(HEAD_DIM)
    return jnp.stack([rot * qs, rot, ident, ident * qs])


def _lambda(lq1_ref, lk1_ref, lq2_ref, lk2_ref, lam_init):
    return (jnp.exp(jnp.sum(lq1_ref[...] * lk1_ref[...], axis=-1, keepdims=True))
            - jnp.exp(jnp.sum(lq2_ref[...] * lk2_ref[...], axis=-1, keepdims=True)) + lam_init)


def _attn_kernel(*refs, n_seg, lam_init, rb, cast):
    q_ref = refs[0]
    seg_refs = refs[1:1 + 2 * n_seg]
    if cast:
        (lq1_ref, lk1_ref, lq2_ref, lk2_ref, gs_ref, w1n_ref, w2n_ref,
         o_ref, w1o_ref, w2o_ref, kk_ref, va_ref) = refs[1 + 2 * n_seg:]
        w1o_ref[...] = w1n_ref[...].astype(BF16)
        w2o_ref[...] = w2n_ref[...].astype(BF16)
    else:
        lq1_ref, lk1_ref, lq2_ref, lk2_ref, gs_ref, o_ref, kk_ref, va_ref = refs[1 + 2 * n_seg:]
    hb, q_len, _ = q_ref.shape
    n_blocks = q_len // rb
    lam = _lambda(lq1_ref, lk1_ref, lq2_ref, lk2_ref, lam_init)
    gs = gs_ref[...]
    lane = lax.broadcasted_iota(jnp.int32, (rb, LANES), 1)
    va_ref[:, V_HEAD_DIM:] = jnp.ones((va_ref.shape[0], V_HEAD_DIM), BF16)

    for h in range(hb):
        off = 0
        for s in range(n_seg):
            n = seg_refs[2 * s].shape[1]
            kk_ref[off:off + n, :] = seg_refs[2 * s][h]
            va_ref[off:off + n, 0:V_HEAD_DIM] = seg_refs[2 * s + 1][h]
            off += n

        def scores(r):
            q = q_ref[h, r * rb:(r + 1) * rb, :]
            qq = jnp.concatenate([jnp.where(lane < HEAD_DIM, q, jnp.zeros_like(q)),
                                  jnp.where(lane >= HEAD_DIM, q, jnp.zeros_like(q))], axis=0)
            return lax.dot_general(qq, kk_ref[...], (((1,), (1,)), ((), ())),
                                   preferred_element_type=F32)

        def finish(r, s):
            p = jnp.exp2(s - jnp.max(s, axis=-1, keepdims=True)).astype(BF16)
            acc = jnp.dot(p, va_ref[...], preferred_element_type=F32)
            ratio = acc[:, :V_HEAD_DIM] / acc[:, V_HEAD_DIM:]
            o = ratio[:rb] - lam * ratio[rb:]
            o = _rms(o, gs, SUBLN_EPS) * (1.0 - lam_init)
            o_ref[r * rb:(r + 1) * rb, h * V_HEAD_DIM:(h + 1) * V_HEAD_DIM] = o.astype(BF16)

        s_next = scores(0)
        for r in range(n_blocks):
            s_cur = s_next
            if r + 1 < n_blocks:
                s_next = scores(r + 1)
            finish(r, s_cur)


def _attention(q_arr, q_row0, q_len, segs, lam_params, gs, *, ia, lam_init, n_batch, hb,
               cast=None):
    n_heads = q_arr.shape[0] // 3
    assert n_heads % hb == 0 and q_row0 % q_len == 0
    nk = sum(n for _, _, n in segs)
    q_blk0 = q_row0 // q_len
    in_specs = [pl.BlockSpec((hb, q_len, LANES), lambda b, h: (h, q_blk0 + b, 0))]
    args = [q_arr]
    for arr, row0, n in segs:
        assert row0 % n == 0 and arr.shape[0] == 3 * n_heads
        for part in (1, 2):
            in_specs.append(pl.BlockSpec(
                (hb, n, LANES),
                lambda b, h, blk0=row0 // n, col0=part * n_heads // hb: (col0 + h, blk0 + b, 0)))
            args.append(arr)
    in_specs += [_vec_spec(HEAD_DIM, ia)] * 4
    in_specs.append(_vec_spec(V_HEAD_DIM, ia))
    args += [*lam_params, gs]
    out_shape = [jax.ShapeDtypeStruct((n_batch * q_len, n_heads * V_HEAD_DIM), BF16)]
    out_specs = [pl.BlockSpec((q_len, hb * V_HEAD_DIM), lambda b, h: (b, h))]
    n_hg = n_heads // hb
    if cast is not None:
        layer, stacks = cast
        for w in stacks:
            _, rows, cols = w.shape
            assert rows % (n_batch * n_hg) == 0
            rblk = rows // (n_batch * n_hg)
            in_specs.append(pl.BlockSpec((None, rblk, cols),
                                         lambda b, h: (layer, b * n_hg + h, 0)))
            args.append(w)
            out_shape.append(jax.ShapeDtypeStruct((rows, cols), BF16))
            out_specs.append(pl.BlockSpec((rblk, cols), lambda b, h: (b * n_hg + h, 0)))
    outs = pl.pallas_call(
        functools.partial(_attn_kernel, n_seg=len(segs), lam_init=lam_init,
                          rb=_tile(128, q_len), cast=cast is not None),
        out_shape=out_shape,
        grid=(n_batch, n_hg),
        in_specs=in_specs,
        out_specs=out_specs,
        scratch_shapes=[pltpu.VMEM((nk, LANES), BF16), pltpu.VMEM((nk, 2 * V_HEAD_DIM), BF16)],
        compiler_params=_cparams(2),
        name="diff_attn",
    )(*args)
    return outs[0] if cast is None else outs


def _proj_res_kernel(*refs, tm, seq, n_batch, n_lat_tiles, has_ctx, split_x):
    refs = list(refs)
    inl_ref = refs.pop(0)
    inc_ref = refs.pop(0) if has_ctx else None
    xl_ref = refs.pop(0)
    xc_ref = refs.pop(0) if split_x else xl_ref
    mod_ref, g_ref, w_ref, o_ref = refs
    i = pl.program_id(0)
    d = o_ref.shape[1]

    def body(inp_ref, x_ref):
        y = jnp.dot(inp_ref[...], w_ref[...], preferred_element_type=F32)
        r = jnp.minimum(lax.div(i * tm, seq), n_batch)
        gate = _mod_row(mod_ref, r, 2, d)
        o_ref[...] = x_ref[...] + gate * _rms(y, g_ref[...], NORM_EPS)

    if has_ctx:
        @pl.when(i < n_lat_tiles)
        def _():
            body(inl_ref, xl_ref)

        @pl.when(i >= n_lat_tiles)
        def _():
            body(inc_ref, xc_ref)
    else:
        body(inl_ref, xl_ref)


def _proj_res(inp_l, inp_c, x_l, x_c, mod, g, w, *, layer, iw, seq, n_batch):
    d = x_l.shape[1]
    n_lat_rows = inp_l.shape[0]
    has_ctx = inp_c is not None
    split_x = x_c is not None
    assert has_ctx or not split_x
    rows = n_lat_rows + (inp_c.shape[0] if has_ctx else 0)
    tm = _tile(512, seq, *([inp_c.shape[0]] if has_ctx else []))
    n_lat_tiles = n_lat_rows // tm
    kin = inp_l.shape[1]
    lat_index = lambda i: (jnp.minimum(i, n_lat_tiles - 1), 0)
    ctx_index = lambda i: (jnp.maximum(i - n_lat_tiles, 0), 0)
    in_specs = [pl.BlockSpec((tm, kin), lat_index)]
    args = [inp_l]
    if has_ctx:
        in_specs.append(pl.BlockSpec((tm, kin), ctx_index))
        args.append(inp_c)
    if split_x:
        in_specs += [pl.BlockSpec((tm, d), lat_index), pl.BlockSpec((tm, d), ctx_index)]
        args += [x_l, x_c]
    else:
        in_specs.append(pl.BlockSpec((tm, d), lambda i: (i, 0)))
        args.append(x_l)
    in_specs += [
        _mod_spec(mod, layer, 1),
        _vec_spec(d, layer),
        pl.BlockSpec((None,) + w.shape[1:], lambda i: (iw, 0, 0)),
    ]
    args += [mod, g, w]
    kern = functools.partial(_proj_res_kernel, tm=tm, seq=seq, n_batch=n_batch,
                             n_lat_tiles=n_lat_tiles, has_ctx=has_ctx, split_x=split_x)
    return pl.pallas_call(
        kern,
        out_shape=jax.ShapeDtypeStruct((rows, d), F32),
        grid=(rows // tm,),
        in_specs=in_specs,
        out_specs=pl.BlockSpec((tm, d), lambda i: (i, 0)),
        compiler_params=_cparams(1),
        name="proj_res",
    )(*args)


def _mlp_kernel(*refs, tm, rb, seq, n_batch, cast_next, single_chunk):
    if cast_next:
        (x_ref, mod_ref, gpre_ref, gpost_ref, w1_ref, w2_ref, w1n_ref, w2n_ref,
         o_ref, w1o_ref, w2o_ref, u_ref) = refs
    else:
        x_ref, mod_ref, gpre_ref, gpost_ref, w1_ref, w2_ref, o_ref, u_ref = refs
    i = pl.program_id(0)
    j = pl.program_id(1)
    d = x_ref.shape[1]
    r = jnp.minimum(lax.div(i * tm, seq), n_batch)

    @pl.when(j == 0)
    def _():
        u_ref[...] = _prenorm(x_ref[...], gpre_ref[...], mod_ref, r, 3, d).astype(BF16)

    def hidden(b):
        h = jnp.dot(u_ref[b * rb:(b + 1) * rb, :], w1_ref[...], preferred_element_type=F32)
        h = jnp.maximum(h, 0.0)
        return (h * h).astype(BF16)

    def chunk(first, last):
        n_blocks = tm // rb
        if last:
            gg = gpost_ref[...] * _mod_row(mod_ref, r, 5, d)
        h_next = hidden(0)
        for b in range(n_blocks):
            h_cur = h_next
            if b + 1 < n_blocks:
                h_next = hidden(b + 1)
            rows = slice(b * rb, (b + 1) * rb)
            y = jnp.dot(h_cur, w2_ref[...], preferred_element_type=F32)
            if not first:
                y = o_ref[rows, :] + y
            if last:
                inv = lax.rsqrt(jnp.mean(y * y, axis=-1, keepdims=True) + NORM_EPS)
                y = x_ref[rows, :] + (y * inv) * gg
            o_ref[rows, :] = y
        if cast_next:
            w1o_ref[...] = w1n_ref[...].astype(BF16)
            w2o_ref[...] = w2n_ref[...].astype(BF16)

    n_j = pl.num_programs(1)

    @pl.when(j == 0)
    def _():
        chunk(True, single_chunk)

    if not single_chunk:
        @pl.when(jnp.logical_and(j > 0, j < n_j - 1))
        def _():
            chunk(False, False)

        @pl.when(j == n_j - 1)
        def _():
            chunk(False, True)


CAST_BLOCK_BYTES = 1024 * 1024


def _mlp(xu, mod, gpre, gpost, w1, w2, w_next, *, layer, seq, n_batch, rows):
    d = xu.shape[1]
    dff = w1.shape[1]
    tm = _tile(1024, seq, *([xu.shape[0] - n_batch * seq] if rows > n_batch * seq else []))
    tf = _tile(512, dff)
    n_j = dff // tf
    n_steps = (rows // tm) * n_j
    cast_next = w_next is not None
    kern = functools.partial(_mlp_kernel, tm=tm, rb=_tile(512, tm), seq=seq, n_batch=n_batch,
                             cast_next=cast_next, single_chunk=n_j == 1)
    in_specs = [
        pl.BlockSpec((tm, d), lambda i, j: (i, 0)),
        _mod_spec(mod, layer, 2),
        _vec_spec(d, layer),
        _vec_spec(d, layer),
        pl.BlockSpec((d, tf), lambda i, j: (0, j)),
        pl.BlockSpec((tf, d), lambda i, j: (j, 0)),
    ]
    args = [xu, mod, gpre, gpost, w1, w2]
    out_shape = [jax.ShapeDtypeStruct((rows, d), F32)]
    out_specs = [pl.BlockSpec((tm, d), lambda i, j: (i, 0))]
    if cast_next:
        r1 = _tile(CAST_BLOCK_BYTES // (4 * dff), d)
        r2 = _tile(CAST_BLOCK_BYTES // (4 * d), dff)
        c1, c2 = d // r1, dff // r2
        assert c1 + c2 <= n_steps

        def blk1(i, j):
            return jnp.minimum(i * n_j + j, c1 - 1)

        def blk2(i, j):
            return jnp.clip(i * n_j + j - c1, 0, c2 - 1)

        in_specs += [pl.BlockSpec((None, r1, dff), lambda i, j: (layer + 1, blk1(i, j), 0)),
                     pl.BlockSpec((None, r2, d), lambda i, j: (layer + 1, blk2(i, j), 0))]
        args += list(w_next)
        out_shape += [jax.ShapeDtypeStruct((d, dff), BF16), jax.ShapeDtypeStruct((dff, d), BF16)]
        out_specs += [pl.BlockSpec((r1, dff), lambda i, j: (blk1(i, j), 0)),
                      pl.BlockSpec((r2, d), lambda i, j: (blk2(i, j), 0))]
    outs = pl.pallas_call(
        kern,
        out_shape=out_shape,
        grid=(rows // tm, n_j),
        in_specs=in_specs,
        out_specs=out_specs,
        scratch_shapes=[pltpu.VMEM((tm, d), BF16)],
        compiler_params=_cparams(2),
        name="sq_relu_mlp",
    )(*args)
    return outs if cast_next else (outs[0], None, None)


def _dft_tables(n):
    b = max(f for f in range(1, math.isqrt(n) + 1) if n % f == 0)
    a = n // b
    j = jnp.arange(n, dtype=jnp.int32)[:, None]
    ang1 = ((j * jnp.arange(a, dtype=jnp.int32)[None, :]) % a).astype(F32) * (2.0 * math.pi / a)
    ang0 = ((j * jnp.arange(b, dtype=jnp.int32)[None, :]) % n).astype(F32) * (2.0 * math.pi / n)
    c1, s1 = jnp.cos(ang1)[:, :, None], jnp.sin(ang1)[:, :, None]
    c0, s0 = jnp.cos(ang0)[:, None, :], jnp.sin(ang0)[:, None, :]
    scale = 1.0 / math.sqrt(n)
    cos = (c1 * c0 - s1 * s0).reshape(n, n) * scale
    sin = (s1 * c0 + c1 * s0).reshape(n, n) * scale
    return cos, sin


def _fourier_ch_kernel(x_ref, mod_ref, g_ref, cs_ref, a_ref, b_ref, *, tm, seq, n_batch):
    i = pl.program_id(0)
    d = x_ref.shape[1]
    cg = d // N_FFT_GROUPS
    r = jnp.minimum(lax.div(i * tm, seq), n_batch)
    u = _prenorm(x_ref[...], g_ref[...], mod_ref, r, 0, d).astype(BF16)
    for g in range(N_FFT_GROUPS):
        ab = jnp.dot(u[:, g * cg:(g + 1) * cg], cs_ref[...], preferred_element_type=F32)
        a_ref[:, g * cg:(g + 1) * cg] = ab[:, :cg].astype(BF16)
        b_ref[:, g * cg:(g + 1) * cg] = ab[:, cg:].astype(BF16)


def _fourier_channels(xu, mod, g, cs, *, layer, seq, n_batch, rows):
    d = xu.shape[1]
    tm = _tile(512, seq, *([xu.shape[0] - n_batch * seq] if rows > n_batch * seq else []))
    kern = functools.partial(_fourier_ch_kernel, tm=tm, seq=seq, n_batch=n_batch)
    return pl.pallas_call(
        kern,
        out_shape=[jax.ShapeDtypeStruct((rows, d), BF16)] * 2,
        grid=(rows // tm,),
        in_specs=[
            pl.BlockSpec((tm, d), lambda i: (i, 0)),
            _mod_spec(mod, layer, 1),
            _vec_spec(d, layer),
            pl.BlockSpec(cs.shape, lambda i: (0, 0)),
        ],
        out_specs=[pl.BlockSpec((tm, d), lambda i: (i, 0))] * 2,
        compiler_params=_cparams(1),
        name="fourier_channels",
    )(xu, mod, g, cs)


def _fourier_seq_kernel(tc_ref, tms_ref, a_ref, b_ref, o_ref, *, tmm):
    n = tc_ref.shape[0]
    for mi in range(n // tmm):
        rows = slice(mi * tmm, (mi + 1) * tmm)
        y = (jnp.dot(tc_ref[rows, :], a_ref[...], preferred_element_type=F32)
             + jnp.dot(tms_ref[rows, :], b_ref[...], preferred_element_type=F32))
        o_ref[rows, :] = y.astype(BF16)


def _fourier_seq(a, b, tc, tms, *, n_batch, length, row0):
    d = a.shape[1]
    tn = _tile(512, d)
    tmm = _tile(512, length)
    blk0 = row0 // length
    assert row0 % length == 0
    ab_spec = pl.BlockSpec((length, tn), lambda bi, ni: (blk0 + bi, ni))
    tab_spec = pl.BlockSpec((length, length), lambda bi, ni: (0, 0))
    return pl.pallas_call(
        functools.partial(_fourier_seq_kernel, tmm=tmm),
        out_shape=jax.ShapeDtypeStruct((n_batch * length, d), BF16),
        grid=(n_batch, d // tn),
        in_specs=[tab_spec, tab_spec, ab_spec, ab_spec],
        out_specs=pl.BlockSpec((length, tn), lambda bi, ni: (bi, ni)),
        compiler_params=_cparams(2),
        name="fourier_seq",
    )(tc, tms, a, b)


def _pool_kernel(x_ref, xp_ref, xn_ref, mod_ref, gpre_ref, gpost_ref, wp_ref, ps_ref, o_ref,
                 ext_ref, y_ref, *, tm, seq, ctx_len, n_batch, n_lat_rows):
    i = pl.program_id(0)
    d = x_ref.shape[1]
    cg = d // len(POOL_WINDOWS)
    row0 = i * tm
    is_lat = row0 < n_lat_rows
    slen = jnp.where(is_lat, seq, ctx_len)
    pos0 = jnp.where(is_lat, lax.rem(row0, seq), lax.rem(row0 - n_lat_rows, ctx_len))
    r = jnp.minimum(lax.div(row0, seq), n_batch)
    g_pre = gpre_ref[...]
    keep_prev = jnp.where(pos0 > 0, 1.0, 0.0).astype(F32)
    keep_next = jnp.where(pos0 + tm < slen, 1.0, 0.0).astype(F32)
    ext_ref[0:POOL_HALO, :] = _prenorm(xp_ref[...], g_pre, mod_ref, r, 0, d) * keep_prev
    ext_ref[POOL_HALO:POOL_HALO + tm, :] = _prenorm(x_ref[...], g_pre, mod_ref, r, 0, d)
    ext_ref[POOL_HALO + tm:, :] = _prenorm(xn_ref[...], g_pre, mod_ref, r, 0, d) * keep_next

    pos = pos0 + lax.broadcasted_iota(jnp.int32, (tm, 1), 0)
    for g, w in enumerate(POOL_WINDOWS):
        cols = slice(g * cg, (g + 1) * cg)
        acc = ext_ref[POOL_HALO - w // 2:POOL_HALO - w // 2 + tm, cols]
        for dlt in range(-w // 2 + 1, w // 2):
            acc = acc + ext_ref[POOL_HALO + dlt:POOL_HALO + dlt + tm, cols]
        lo = jnp.maximum(pos - w // 2, 0)
        hi = jnp.minimum(pos - w // 2 + w, slen)
        mean = acc / (hi - lo).astype(F32)
        diff = (mean - ext_ref[POOL_HALO:POOL_HALO + tm, cols]).astype(BF16)
        y_ref[:, cols] = jnp.dot(diff, wp_ref[g], preferred_element_type=F32) * ps_ref[:, cols]

    gate = _mod_row(mod_ref, r, 2, d)
    o_ref[...] = x_ref[...] + gate * _rms(y_ref[...], gpost_ref[...], NORM_EPS)


def _pool_mix(xu, mod, gpre, gpost, wp, ps, *, layer, ic, seq, ctx_len, n_batch, rows):
    d = xu.shape[1]
    n_lat_rows = n_batch * seq
    tm = _tile(256, seq, *([ctx_len] if rows > n_lat_rows else []))
    assert tm % POOL_HALO == 0 and POOL_HALO >= max(POOL_WINDOWS) // 2
    hb = tm // POOL_HALO
    last_hblk = xu.shape[0] // POOL_HALO - 1
    kern = functools.partial(_pool_kernel, tm=tm, seq=seq, ctx_len=ctx_len, n_batch=n_batch,
                             n_lat_rows=n_lat_rows)
    return pl.pallas_call(
        kern,
        out_shape=jax.ShapeDtypeStruct((rows, d), F32),
        grid=(rows // tm,),
        in_specs=[
            pl.BlockSpec((tm, d), lambda i: (i, 0)),
            pl.BlockSpec((POOL_HALO, d), lambda i: (jnp.maximum(i * hb - 1, 0), 0)),
            pl.BlockSpec((POOL_HALO, d), lambda i: (jnp.minimum((i + 1) * hb, last_hblk), 0)),
            _mod_spec(mod, layer, 1),
            _vec_spec(d, layer),
            _vec_spec(d, layer),
            pl.BlockSpec((None,) + wp.shape[1:], lambda i: (ic, 0, 0, 0)),
            _vec_spec(d, ic),
        ],
        out_specs=pl.BlockSpec((tm, d), lambda i: (i, 0)),
        scratch_shapes=[pltpu.VMEM((tm + 2 * POOL_HALO, d), F32), pltpu.VMEM((tm, d), F32)],
        compiler_params=_cparams(1),
        name="pool_mix",
    )(xu, xu, xu, mod, gpre, gpost, wp, ps)


def kernel(x, c, ctx, c_ctx, w_mod, b_mod, g_mix_pre, g_mix_post, g_mlp_pre, g_mlp_post,
           w_qkv, w_attn_out, lambda_q1, lambda_k1, lambda_q2, lambda_k2, g_subln,
           w_fourier_out, w_pool, pool_scale, w_mlp_in, w_mlp_out):
    n_batch, seq, d = x.shape
    ctx_len = ctx.shape[1]
    depth = w_mod.shape[0]
    n_lat_rows = n_batch * seq
    n_ctx_rows = n_batch * ctx_len
    n_rows = n_lat_rows + n_ctx_rows
    n_heads = w_attn_out.shape[1] // V_HEAD_DIM

    n_cond = -(-(n_batch + 1) // 8) * 8
    cond = jnp.concatenate(
        [c, c_ctx[None, :], jnp.zeros((n_cond - n_batch - 1, d), F32)], axis=0)
    mod = _mod_all(cond, w_mod, b_mod)

    w_o_b = w_attn_out.astype(BF16)
    w_f_b = w_fourier_out.astype(BF16)
    w_p_b = w_pool.astype(BF16)
    w1_b = w2_b = None

    row3 = lambda t: t.reshape(t.shape[0], 1, t.shape[1])
    g_mix_pre, g_mix_post, g_mlp_pre, g_mlp_post = map(
        row3, (g_mix_pre, g_mix_post, g_mlp_pre, g_mlp_post))
    lambda_q1, lambda_k1, lambda_q2, lambda_k2, g_subln, pool_scale = map(
        row3, (lambda_q1, lambda_k1, lambda_q2, lambda_k2, g_subln, pool_scale))
    lam_params = (lambda_q1, lambda_k1, lambda_q2, lambda_k2)

    rope = _rope_tables(seq)
    common = dict(seq=seq, n_batch=n_batch)

    x_l = x.reshape(n_lat_rows, d)
    x_c = ctx.reshape(n_ctx_rows, d)
    xu = None

    ia = ib = ic = 0
    for i in range(depth):
        last = i == depth - 1
        kind = i % N_MIXERS
        out_rows = n_lat_rows if last else n_rows
        if xu is None and kind != 0:
            xu = jnp.concatenate([x_l, x_c], axis=0)
        if kind == 0:
            lam_init = 0.8 - 0.6 * math.exp(-0.3 * i)
            qkw = dict(layer=i, ia=ia, **common)
            akw = dict(ia=ia, lam_init=lam_init, n_batch=n_batch)
            if xu is None:
                qkv_l = _qkv_proj(x_l, mod, g_mix_pre, w_qkv, rope, row0=0, **qkw)
                qkv_c = _qkv_proj(x_c, mod, g_mix_pre, w_qkv, rope, row0=n_lat_rows, **qkw)
                ctx_row0 = 0
            else:
                qkv_l = qkv_c = _qkv_proj(xu, mod, g_mix_pre, w_qkv, rope, row0=0, **qkw)
                ctx_row0 = n_lat_rows
            ctx_seg = (qkv_c, ctx_row0, ctx_len)
            cast = (i, [w_mlp_in, w_mlp_out]) if w1_b is None else None
            o_l = _attention(qkv_l, 0, seq, [(qkv_l, 0, seq), ctx_seg], lam_params, g_subln,
                             hb=1, cast=cast, **akw)
            if cast is not None:
                o_l, w1_b, w2_b = o_l
            o_c = None if last else _attention(
                qkv_c, ctx_row0, ctx_len, [ctx_seg], lam_params, g_subln, hb=n_heads, **akw)
            if xu is None:
                xu = _proj_res(o_l, o_c, x_l, None if last else x_c, mod, g_mix_post, w_o_b,
                               layer=i, iw=ia, **common)
            else:
                xu = _proj_res(o_l, o_c, xu, None, mod, g_mix_post, w_o_b,
                               layer=i, iw=ia, **common)
            ia += 1
        elif kind == 1:
            cg = d // N_FFT_GROUPS
            cc, sc = _dft_tables(cg)
            cs = jnp.concatenate([cc, sc], axis=1).astype(BF16)
            a, b = _fourier_channels(xu, mod, g_mix_pre, cs, layer=i, rows=out_rows, **common)
            tc, ts = _dft_tables(seq)
            y_l = _fourier_seq(a, b, tc.astype(BF16), (-ts).astype(BF16),
                               n_batch=n_batch, length=seq, row0=0)
            y_c = None
            if not last:
                tcc, tsc = _dft_tables(ctx_len)
                y_c = _fourier_seq(a, b, tcc.astype(BF16), (-tsc).astype(BF16),
                                   n_batch=n_batch, length=ctx_len, row0=n_lat_rows)
            xu = _proj_res(y_l, y_c, xu, None, mod, g_mix_post, w_f_b, layer=i, iw=ib, **common)
            ib += 1
        else:
            xu = _pool_mix(xu, mod, g_mix_pre, g_mix_post, w_p_b, pool_scale, layer=i, ic=ic,
                           ctx_len=ctx_len, rows=out_rows, **common)
            ic += 1
        w_next = None if last else (w_mlp_in, w_mlp_out)
        xu, w1_b, w2_b = _mlp(xu, mod, g_mlp_pre, g_mlp_post, w1_b, w2_b, w_next, layer=i,
                              rows=out_rows, **common)
    return xu[:n_lat_rows].reshape(n_batch, seq, d)
```

```python
import functools
import math

import jax
import jax.numpy as jnp
from jax import lax
from jax.experimental import pallas as pl
from jax.experimental.pallas import tpu as pltpu

F32 = jnp.float32
BF16 = jnp.bfloat16

GRID_W = 64
N_MIXERS = 3
HEAD_DIM = 64
V_HEAD_DIM = 2 * HEAD_DIM
ROPE_BASE = 10000.0
AXIS_ROT = HEAD_DIM // 2
SUBLN_EPS = 1e-5
N_FFT_GROUPS = 4
POOL_WINDOWS = (2, 4, 8, 16)
NORM_EPS = 1e-6
N_MOD = 6
POOL_HALO = 8

LANES = 128
V7X_VMEM_BYTES = 64 * 1024 * 1024
VMEM_LIMIT = V7X_VMEM_BYTES - 4 * 1024 * 1024


def _cparams(n_axes):
    return pltpu.CompilerParams(
        dimension_semantics=("arbitrary",) * n_axes, vmem_limit_bytes=VMEM_LIMIT)


def _tile(pref, *counts):
    t = min(pref, *counts)
    while t > 8 and any(c % t for c in counts):
        t -= 8
    assert t >= 8 and all(c % t == 0 for c in counts), (pref, counts)
    return t


def _rms(x, g, eps):
    return x * lax.rsqrt(jnp.mean(x * x, axis=-1, keepdims=True) + eps) * g


def _mod_row(mod_ref, r, k, d):
    return mod_ref[pl.ds(r, 1), k * d:(k + 1) * d]


def _prenorm(x, g, mod_ref, r, k_shift, d):
    sh = _mod_row(mod_ref, r, k_shift, d)
    sc = _mod_row(mod_ref, r, k_shift + 1, d)
    inv = lax.rsqrt(jnp.mean(x * x, axis=-1, keepdims=True) + NORM_EPS)
    return (x * inv) * (g * (1.0 + sc)) + sh


def _mod_spec(mod, layer, n_axes):
    index = layer if mod.shape[0] > 1 else 0
    return pl.BlockSpec((None,) + mod.shape[1:], lambda *_: (index, 0, 0))


def _vec_spec(width, index):
    return pl.BlockSpec((None, 1, width), lambda *_: (index, 0, 0))


def _mod_kernel(cond_ref, w_ref, b_ref, o_ref):
    s = jax.nn.silu(cond_ref[...]).astype(BF16)
    o_ref[...] = jnp.dot(s, w_ref[...].astype(BF16), preferred_element_type=F32) + b_ref[...]


def _mod_layers(cond, w_mod, b_mod, n_layers):
    depth, d, n = w_mod.shape
    rows = cond.shape[0]
    tn = _tile(1024, n)
    return pl.pallas_call(
        _mod_kernel,
        out_shape=jax.ShapeDtypeStruct((n_layers, rows, n), F32),
        grid=(n_layers, n // tn),
        in_specs=[
            pl.BlockSpec((rows, d), lambda l, j: (0, 0)),
            pl.BlockSpec((None, d, tn), lambda l, j: (l, 0, j)),
            pl.BlockSpec((None, 1, tn), lambda l, j: (l, 0, j)),
        ],
        out_specs=pl.BlockSpec((None, rows, tn), lambda l, j: (l, 0, j)),
        compiler_params=_cparams(2),
        name="ada_mod",
    )(cond, w_mod, b_mod)


def _qkv_kernel(x_ref, mod_ref, g_ref, w_ref, tab_ref, o_ref, u_ref, *, tm, tn, seq, n_batch, row0):
    i = pl.program_id(0)
    d = x_ref.shape[1]

    @pl.when(pl.program_id(1) == 0)
    def _():
        r = jnp.minimum(lax.div(row0 + i * tm, seq), n_batch)
        u_ref[...] = _prenorm(x_ref[...], g_ref[...], mod_ref, r, 0, d).astype(BF16)

    rb = _tile(256, tm)
    w = w_ref[...].astype(BF16)
    for b in range(tm // rb):
        rows = slice(b * rb, (b + 1) * rb)
        y = jnp.dot(u_ref[rows, :], w, preferred_element_type=F32)
        cos = tab_ref[0, rows, :]
        sa = tab_ref[1, rows, :]
        sb = tab_ref[2, rows, :]
        for c in range(tn // LANES):
            yc = y[:, c * LANES:(c + 1) * LANES]
            rot = (yc * cos + pltpu.roll(yc, LANES - AXIS_ROT // 2, 1) * sa
                   + pltpu.roll(yc, AXIS_ROT // 2, 1) * sb)
            o_ref[c, rows, :] = rot.astype(BF16)


def _qkv_proj(xa, mod, g, w, rope, *, layer, ia, seq, n_batch, row0):
    rows, d = xa.shape
    n = w.shape[2]
    n_lat_rows = n_batch * seq
    n_lat_here = max(0, min(rows, n_lat_rows - row0))
    tm = _tile(1024, seq, *[c for c in (n_lat_here, rows - n_lat_here) if c])
    tn = _tile(512, n // 3)
    assert tn % LANES == 0 and row0 % tm == 0
    lat_tiles_per_seq = seq // tm
    n_lat_tiles = n_lat_here // tm
    n_q_j = n // 3 // tn

    def tab_index(i, j):
        lat = jnp.where(j < n_q_j, 0, jnp.where(j < 2 * n_q_j, 1, 2))
        ctx = jnp.where(j < n_q_j, 3, 2)
        return (jnp.where(i < n_lat_tiles, lat, ctx), 0, (row0 // tm + i) % lat_tiles_per_seq, 0)

    kern = functools.partial(_qkv_kernel, tm=tm, tn=tn, seq=seq, n_batch=n_batch, row0=row0)
    return pl.pallas_call(
        kern,
        out_shape=jax.ShapeDtypeStruct((n // LANES, rows, LANES), BF16),
        grid=(rows // tm, n // tn),
        in_specs=[
            pl.BlockSpec((tm, d), lambda i, j: (i, 0)),
            _mod_spec(mod, layer, 2),
            _vec_spec(d, layer),
            pl.BlockSpec((None, d, tn), lambda i, j: (ia, 0, j)),
            pl.BlockSpec((None, 3, tm, LANES), tab_index),
        ],
        out_specs=pl.BlockSpec((tn // LANES, tm, LANES), lambda i, j: (j, i, 0)),
        scratch_shapes=[pltpu.VMEM((tm, d), BF16)],
        compiler_params=_cparams(2),
        name="qkv_proj",
    )(xa, mod, g, w, rope)


def _rope_tables(seq):
    rows = seq // GRID_W
    row = jnp.repeat(jnp.arange(rows), GRID_W).astype(F32)
    col = jnp.tile(jnp.arange(GRID_W), rows).astype(F32)
    n_freq = AXIS_ROT // 2
    inv = 1.0 / (ROPE_BASE ** (jnp.arange(n_freq, dtype=F32) / n_freq))
    ang_r = row[:, None] * inv
    ang_c = col[:, None] * inv
    cr, sr, cc, sc = jnp.cos(ang_r), jnp.sin(ang_r), jnp.cos(ang_c), jnp.sin(ang_c)
    z = jnp.zeros_like(sr)
    reps = LANES // HEAD_DIM
    cos = jnp.tile(jnp.concatenate([cr, cr, cc, cc], axis=1), (1, reps))
    sa = jnp.tile(jnp.concatenate([-sr, z, -sc, z], axis=1), (1, reps))
    sb = jnp.tile(jnp.concatenate([z, sr, z, sc], axis=1), (1, reps))
    rot = jnp.stack([cos, sa, sb])
    ident = jnp.stack([jnp.ones_like(cos), jnp.zeros_like(cos), jnp.zeros_like(cos)])
    qs = math.log2(math.e) / math.sqrt(HEAD_DIM)
    return jnp.stack([rot * qs, rot, ident, ident * qs])


def _lambda(lq1_ref, lk1_ref, lq2_ref, lk2_ref, lam_init):
    return (jnp.exp(jnp.sum(lq1_ref[...] * lk1_ref[...], axis=-1, keepdims=True))
            - jnp.exp(jnp.sum(lq2_ref[...] * lk2_ref[...], axis=-1, keepdims=True)) + lam_init)


def _attn_kernel(*refs, n_seg, lam_init, rb, n_cast, n_mod):
    refs = list(refs)
    take = lambda n: [refs.pop(0) for _ in range(n)]
    (q_ref,) = take(1)
    seg_refs = take(2 * n_seg)
    lq1_ref, lk1_ref, lq2_ref, lk2_ref, gs_ref = take(5)
    cast_in = take(n_cast)
    cond_ref, = take(1) if n_mod else (None,)
    wm_refs, bm_refs = take(n_mod), take(n_mod)
    (o_ref,) = take(1)
    cast_out, mod_out = take(n_cast), take(n_mod)
    kk_ref, va_ref = refs

    for w_in, w_out in zip(cast_in, cast_out):
        w_out[...] = w_in[...].astype(BF16)
    if n_mod:
        sc = jax.nn.silu(cond_ref[...]).astype(BF16)
        for wm_ref, bm_ref, mo_ref in zip(wm_refs, bm_refs, mod_out):
            mo_ref[...] = (jnp.dot(sc, wm_ref[...].astype(BF16), preferred_element_type=F32)
                           + bm_ref[...])

    hb, q_len, _ = q_ref.shape
    n_blocks = q_len // rb
    lam = _lambda(lq1_ref, lk1_ref, lq2_ref, lk2_ref, lam_init)
    gs = gs_ref[...]
    lane = lax.broadcasted_iota(jnp.int32, (rb, LANES), 1)
    va_ref[:, V_HEAD_DIM:] = jnp.ones((va_ref.shape[0], V_HEAD_DIM), BF16)

    for h in range(hb):
        off = 0
        for s in range(n_seg):
            n = seg_refs[2 * s].shape[1]
            kk_ref[off:off + n, :] = seg_refs[2 * s][h]
            va_ref[off:off + n, 0:V_HEAD_DIM] = seg_refs[2 * s + 1][h]
            off += n

        def scores(r):
            q = q_ref[h, r * rb:(r + 1) * rb, :]
            qq = jnp.concatenate([jnp.where(lane < HEAD_DIM, q, jnp.zeros_like(q)),
                                  jnp.where(lane >= HEAD_DIM, q, jnp.zeros_like(q))], axis=0)
            return lax.dot_general(qq, kk_ref[...], (((1,), (1,)), ((), ())),
                                   preferred_element_type=F32)

        def finish(r, s):
            p = jnp.exp2(s - jnp.max(s, axis=-1, keepdims=True)).astype(BF16)
            acc = jnp.dot(p, va_ref[...], preferred_element_type=F32)
            ratio = acc[:, :V_HEAD_DIM] / acc[:, V_HEAD_DIM:]
            o = ratio[:rb] - lam * ratio[rb:]
            o = _rms(o, gs, SUBLN_EPS) * (1.0 - lam_init)
            o_ref[r * rb:(r + 1) * rb, h * V_HEAD_DIM:(h + 1) * V_HEAD_DIM] = o.astype(BF16)

        s_next = scores(0)
        for r in range(n_blocks):
            s_cur = s_next
            if r + 1 < n_blocks:
                s_next = scores(r + 1)
            finish(r, s_cur)


def _attention(q_arr, q_row0, q_len, segs, lam_params, gs, *, ia, lam_init, n_batch, hb,
               cast=None, mods=None):
    n_heads = q_arr.shape[0] // 3
    assert n_heads % hb == 0 and q_row0 % q_len == 0
    nk = sum(n for _, _, n in segs)
    q_blk0 = q_row0 // q_len
    in_specs = [pl.BlockSpec((hb, q_len, LANES), lambda b, h: (h, q_blk0 + b, 0))]
    args = [q_arr]
    for arr, row0, n in segs:
        assert row0 % n == 0 and arr.shape[0] == 3 * n_heads
        for part in (1, 2):
            in_specs.append(pl.BlockSpec(
                (hb, n, LANES),
                lambda b, h, blk0=row0 // n, col0=part * n_heads // hb: (col0 + h, blk0 + b, 0)))
            args.append(arr)
    in_specs += [_vec_spec(HEAD_DIM, ia)] * 4
    in_specs.append(_vec_spec(V_HEAD_DIM, ia))
    args += [*lam_params, gs]
    out_shape = [jax.ShapeDtypeStruct((n_batch * q_len, n_heads * V_HEAD_DIM), BF16)]
    out_specs = [pl.BlockSpec((q_len, hb * V_HEAD_DIM), lambda b, h: (b, h))]
    n_hg = n_heads // hb
    n_steps = n_batch * n_hg
    step = lambda b, h: b * n_hg + h
    n_cast = n_mod = 0
    if cast is not None:
        layer, stacks = cast
        n_cast = len(stacks)
        for w in stacks:
            _, rows, cols = w.shape
            assert rows % n_steps == 0
            rblk = rows // n_steps
            in_specs.append(pl.BlockSpec((None, rblk, cols),
                                         lambda b, h, layer=layer: (layer, step(b, h), 0)))
            args.append(w)
            out_shape.append(jax.ShapeDtypeStruct((rows, cols), BF16))
            out_specs.append(pl.BlockSpec((rblk, cols), lambda b, h: (step(b, h), 0)))
    if mods is not None:
        cond, w_mod, b_mod3, layers = mods
        n_mod = len(layers)
        _, dm, nm = w_mod.shape
        tnm = min(t for t in range(LANES, nm + 1, LANES) if nm % t == 0 and nm // t <= n_steps)
        blk = lambda b, h: jnp.minimum(step(b, h), nm // tnm - 1)
        in_specs.append(pl.BlockSpec(cond.shape, lambda b, h: (0, 0)))
        args.append(cond)
        for l in layers:
            in_specs.append(pl.BlockSpec((None, dm, tnm), lambda b, h, l=l: (l, 0, blk(b, h))))
            args.append(w_mod)
        for l in layers:
            in_specs.append(pl.BlockSpec((None, 1, tnm), lambda b, h, l=l: (l, 0, blk(b, h))))
            args.append(b_mod3)
            out_shape.append(jax.ShapeDtypeStruct((1, cond.shape[0], nm), F32))
            out_specs.append(pl.BlockSpec((None, cond.shape[0], tnm),
                                          lambda b, h: (0, 0, blk(b, h))))
    outs = pl.pallas_call(
        functools.partial(_attn_kernel, n_seg=len(segs), lam_init=lam_init,
                          rb=_tile(128, q_len), n_cast=n_cast, n_mod=n_mod),
        out_shape=out_shape,
        grid=(n_batch, n_hg),
        in_specs=in_specs,
        out_specs=out_specs,
        scratch_shapes=[pltpu.VMEM((nk, LANES), BF16), pltpu.VMEM((nk, 2 * V_HEAD_DIM), BF16)],
        compiler_params=_cparams(2),
        name="diff_attn",
    )(*args)
    return outs[0], outs[1:1 + n_cast], outs[1 + n_cast:]


def _proj_res_kernel(*refs, tm, seq, n_batch, n_lat_tiles, has_ctx, split_x):
    refs = list(refs)
    inl_ref = refs.pop(0)
    inc_ref = refs.pop(0) if has_ctx else None
    xl_ref = refs.pop(0)
    xc_ref = refs.pop(0) if split_x else xl_ref
    mod_ref, g_ref, w_ref, o_ref = refs
    i = pl.program_id(0)
    d = o_ref.shape[1]

    def body(inp_ref, x_ref):
        r = jnp.minimum(lax.div(i * tm, seq), n_batch)
        gg = g_ref[...] * _mod_row(mod_ref, r, 2, d)
        rb = _tile(256, tm)
        for b in range(tm // rb):
            rows = slice(b * rb, (b + 1) * rb)
            y = jnp.dot(inp_ref[rows, :], w_ref[...], preferred_element_type=F32)
            inv = lax.rsqrt(jnp.mean(y * y, axis=-1, keepdims=True) + NORM_EPS)
            o_ref[rows, :] = x_ref[rows, :] + (y * inv) * gg

    if has_ctx:
        @pl.when(i < n_lat_tiles)
        def _():
            body(inl_ref, xl_ref)

        @pl.when(i >= n_lat_tiles)
        def _():
            body(inc_ref, xc_ref)
    else:
        body(inl_ref, xl_ref)


def _proj_res(inp_l, inp_c, x_l, x_c, mod, g, w, *, layer, iw, seq, n_batch):
    d = x_l.shape[1]
    n_lat_rows = inp_l.shape[0]
    has_ctx = inp_c is not None
    split_x = x_c is not None
    assert has_ctx or not split_x
    rows = n_lat_rows + (inp_c.shape[0] if has_ctx else 0)
    tm = _tile(512, seq, *([inp_c.shape[0]] if has_ctx else []))
    n_lat_tiles = n_lat_rows // tm
    kin = inp_l.shape[1]
    lat_index = lambda i: (jnp.minimum(i, n_lat_tiles - 1), 0)
    ctx_index = lambda i: (jnp.maximum(i - n_lat_tiles, 0), 0)
    in_specs = [pl.BlockSpec((tm, kin), lat_index)]
    args = [inp_l]
    if has_ctx:
        in_specs.append(pl.BlockSpec((tm, kin), ctx_index))
        args.append(inp_c)
    if split_x:
        in_specs += [pl.BlockSpec((tm, d), lat_index), pl.BlockSpec((tm, d), ctx_index)]
        args += [x_l, x_c]
    else:
        in_specs.append(pl.BlockSpec((tm, d), lambda i: (i, 0)))
        args.append(x_l)
    in_specs += [
        _mod_spec(mod, layer, 1),
        _vec_spec(d, layer),
        pl.BlockSpec((None,) + w.shape[1:], lambda i: (iw, 0, 0)),
    ]
    args += [mod, g, w]
    kern = functools.partial(_proj_res_kernel, tm=tm, seq=seq, n_batch=n_batch,
                             n_lat_tiles=n_lat_tiles, has_ctx=has_ctx, split_x=split_x)
    return pl.pallas_call(
        kern,
        out_shape=jax.ShapeDtypeStruct((rows, d), F32),
        grid=(rows // tm,),
        in_specs=in_specs,
        out_specs=pl.BlockSpec((tm, d), lambda i: (i, 0)),
        compiler_params=_cparams(1),
        name="proj_res",
    )(*args)


def _mlp_kernel(*refs, tm, rb, seq, n_batch, cast_next, single_chunk):
    if cast_next:
        (x_ref, mod_ref, gpre_ref, gpost_ref, w1_ref, w2_ref, w1n_ref, w2n_ref,
         o_ref, w1o_ref, w2o_ref, u_ref) = refs
    else:
        x_ref, mod_ref, gpre_ref, gpost_ref, w1_ref, w2_ref, o_ref, u_ref = refs
    i = pl.program_id(0)
    j = pl.program_id(1)
    d = x_ref.shape[1]
    r = jnp.minimum(lax.div(i * tm, seq), n_batch)

    @pl.when(j == 0)
    def _():
        u_ref[...] = _prenorm(x_ref[...], gpre_ref[...], mod_ref, r, 3, d).astype(BF16)

    def hidden(b):
        h = jnp.dot(u_ref[b * rb:(b + 1) * rb, :], w1_ref[...], preferred_element_type=F32)
        h = jnp.maximum(h, 0.0)
        return (h * h).astype(BF16)

    def chunk(first, last):
        n_blocks = tm // rb
        if last:
            gg = gpost_ref[...] * _mod_row(mod_ref, r, 5, d)
        h_next = hidden(0)
        for b in range(n_blocks):
            h_cur = h_next
            if b + 1 < n_blocks:
                h_next = hidden(b + 1)
            rows = slice(b * rb, (b + 1) * rb)
            y = jnp.dot(h_cur, w2_ref[...], preferred_element_type=F32)
            if not first:
                y = o_ref[rows, :] + y
            if last:
                inv = lax.rsqrt(jnp.mean(y * y, axis=-1, keepdims=True) + NORM_EPS)
                y = x_ref[rows, :] + (y * inv) * gg
            o_ref[rows, :] = y
        if cast_next:
            w1o_ref[...] = w1n_ref[...].astype(BF16)
            w2o_ref[...] = w2n_ref[...].astype(BF16)

    n_j = pl.num_programs(1)

    @pl.when(j == 0)
    def _():
        chunk(True, single_chunk)

    if not single_chunk:
        @pl.when(jnp.logical_and(j > 0, j < n_j - 1))
        def _():
            chunk(False, False)

        @pl.when(j == n_j - 1)
        def _():
            chunk(False, True)


CAST_BLOCK_BYTES = 1024 * 1024


def _mlp(xu, mod, gpre, gpost, w1, w2, w_next, *, layer, seq, n_batch, rows):
    d = xu.shape[1]
    dff = w1.shape[1]
    tm = _tile(1024, seq, *([xu.shape[0] - n_batch * seq] if rows > n_batch * seq else []))
    tf = _tile(512, dff)
    n_j = dff // tf
    n_steps = (rows // tm) * n_j
    cast_next = w_next is not None
    kern = functools.partial(_mlp_kernel, tm=tm, rb=_tile(512, tm), seq=seq, n_batch=n_batch,
                             cast_next=cast_next, single_chunk=n_j == 1)
    in_specs = [
        pl.BlockSpec((tm, d), lambda i, j: (i, 0)),
        _mod_spec(mod, layer, 2),
        _vec_spec(d, layer),
        _vec_spec(d, layer),
        pl.BlockSpec((d, tf), lambda i, j: (0, j)),
        pl.BlockSpec((tf, d), lambda i, j: (j, 0)),
    ]
    args = [xu, mod, gpre, gpost, w1, w2]
    out_shape = [jax.ShapeDtypeStruct((rows, d), F32)]
    out_specs = [pl.BlockSpec((tm, d), lambda i, j: (i, 0))]
    if cast_next:
        r1 = _tile(CAST_BLOCK_BYTES // (4 * dff), d)
        r2 = _tile(CAST_BLOCK_BYTES // (4 * d), dff)
        c1, c2 = d // r1, dff // r2
        assert c1 + c2 <= n_steps

        def blk1(i, j):
            return jnp.minimum(i * n_j + j, c1 - 1)

        def blk2(i, j):
            return jnp.clip(i * n_j + j - c1, 0, c2 - 1)

        in_specs += [pl.BlockSpec((None, r1, dff), lambda i, j: (layer + 1, blk1(i, j), 0)),
                     pl.BlockSpec((None, r2, d), lambda i, j: (layer + 1, blk2(i, j), 0))]
        args += list(w_next)
        out_shape += [jax.ShapeDtypeStruct((d, dff), BF16), jax.ShapeDtypeStruct((dff, d), BF16)]
        out_specs += [pl.BlockSpec((r1, dff), lambda i, j: (blk1(i, j), 0)),
                      pl.BlockSpec((r2, d), lambda i, j: (blk2(i, j), 0))]
    outs = pl.pallas_call(
        kern,
        out_shape=out_shape,
        grid=(rows // tm, n_j),
        in_specs=in_specs,
        out_specs=out_specs,
        scratch_shapes=[pltpu.VMEM((tm, d), BF16)],
        compiler_params=_cparams(2),
        name="sq_relu_mlp",
    )(*args)
    return outs if cast_next else (outs[0], None, None)


def _dft_tables(n):
    b = max(f for f in range(1, math.isqrt(n) + 1) if n % f == 0)
    a = n // b
    j = jnp.arange(n, dtype=jnp.int32)[:, None]
    ang1 = ((j * jnp.arange(a, dtype=jnp.int32)[None, :]) % a).astype(F32) * (2.0 * math.pi / a)
    ang0 = ((j * jnp.arange(b, dtype=jnp.int32)[None, :]) % n).astype(F32) * (2.0 * math.pi / n)
    c1, s1 = jnp.cos(ang1)[:, :, None], jnp.sin(ang1)[:, :, None]
    c0, s0 = jnp.cos(ang0)[:, None, :], jnp.sin(ang0)[:, None, :]
    scale = 1.0 / math.sqrt(n)
    cos = (c1 * c0 - s1 * s0).reshape(n, n) * scale
    sin = (s1 * c0 + c1 * s0).reshape(n, n) * scale
    return cos, sin


def _fourier_ch_kernel(x_ref, mod_ref, g_ref, cs_ref, a_ref, b_ref, *, tm, seq, n_batch):
    i = pl.program_id(0)
    d = x_ref.shape[1]
    cg = d // N_FFT_GROUPS
    r = jnp.minimum(lax.div(i * tm, seq), n_batch)
    u = _prenorm(x_ref[...], g_ref[...], mod_ref, r, 0, d).astype(BF16)
    for g in range(N_FFT_GROUPS):
        ab = jnp.dot(u[:, g * cg:(g + 1) * cg], cs_ref[...], preferred_element_type=F32)
        a_ref[:, g * cg:(g + 1) * cg] = ab[:, :cg].astype(BF16)
        b_ref[:, g * cg:(g + 1) * cg] = ab[:, cg:].astype(BF16)


def _fourier_channels(xu, mod, g, cs, *, layer, seq, n_batch, rows):
    d = xu.shape[1]
    tm = _tile(512, seq, *([xu.shape[0] - n_batch * seq] if rows > n_batch * seq else []))
    kern = functools.partial(_fourier_ch_kernel, tm=tm, seq=seq, n_batch=n_batch)
    return pl.pallas_call(
        kern,
        out_shape=[jax.ShapeDtypeStruct((rows, d), BF16)] * 2,
        grid=(rows // tm,),
        in_specs=[
            pl.BlockSpec((tm, d), lambda i: (i, 0)),
            _mod_spec(mod, layer, 1),
            _vec_spec(d, layer),
            pl.BlockSpec(cs.shape, lambda i: (0, 0)),
        ],
        out_specs=[pl.BlockSpec((tm, d), lambda i: (i, 0))] * 2,
        compiler_params=_cparams(1),
        name="fourier_channels",
    )(xu, mod, g, cs)


def _fourier_seq_kernel(tc_ref, tms_ref, a_ref, b_ref, o_ref, *, tmm):
    n = tc_ref.shape[0]
    for mi in range(n // tmm):
        rows = slice(mi * tmm, (mi + 1) * tmm)
        y = (jnp.dot(tc_ref[rows, :], a_ref[...], preferred_element_type=F32)
             + jnp.dot(tms_ref[rows, :], b_ref[...], preferred_element_type=F32))
        o_ref[rows, :] = y.astype(BF16)


def _fourier_seq(a, b, tc, tms, *, n_batch, length, row0):
    d = a.shape[1]
    tn = _tile(512, d)
    tmm = _tile(512, length)
    blk0 = row0 // length
    assert row0 % length == 0
    ab_spec = pl.BlockSpec((length, tn), lambda bi, ni: (blk0 + bi, ni))
    tab_spec = pl.BlockSpec((length, length), lambda bi, ni: (0, 0))
    return pl.pallas_call(
        functools.partial(_fourier_seq_kernel, tmm=tmm),
        out_shape=jax.ShapeDtypeStruct((n_batch * length, d), BF16),
        grid=(n_batch, d // tn),
        in_specs=[tab_spec, tab_spec, ab_spec, ab_spec],
        out_specs=pl.BlockSpec((length, tn), lambda bi, ni: (bi, ni)),
        compiler_params=_cparams(2),
        name="fourier_seq",
    )(tc, tms, a, b)


def _pool_kernel(x_ref, xp_ref, xn_ref, mod_ref, gpre_ref, gpost_ref, wp_ref, ps_ref, o_ref,
                 ext_ref, y_ref, *, tm, seq, ctx_len, n_batch, n_lat_rows):
    i = pl.program_id(0)
    d = x_ref.shape[1]
    cg = d // len(POOL_WINDOWS)
    row0 = i * tm
    is_lat = row0 < n_lat_rows
    slen = jnp.where(is_lat, seq, ctx_len)
    pos0 = jnp.where(is_lat, lax.rem(row0, seq), lax.rem(row0 - n_lat_rows, ctx_len))
    r = jnp.minimum(lax.div(row0, seq), n_batch)
    g_pre = gpre_ref[...]
    keep_prev = jnp.where(pos0 > 0, 1.0, 0.0).astype(F32)
    keep_next = jnp.where(pos0 + tm < slen, 1.0, 0.0).astype(F32)
    ext_ref[0:POOL_HALO, :] = _prenorm(xp_ref[...], g_pre, mod_ref, r, 0, d) * keep_prev
    ext_ref[POOL_HALO:POOL_HALO + tm, :] = _prenorm(x_ref[...], g_pre, mod_ref, r, 0, d)
    ext_ref[POOL_HALO + tm:, :] = _prenorm(xn_ref[...], g_pre, mod_ref, r, 0, d) * keep_next

    pos = pos0 + lax.broadcasted_iota(jnp.int32, (tm, 1), 0)
    for g, w in enumerate(POOL_WINDOWS):
        cols = slice(g * cg, (g + 1) * cg)
        acc = ext_ref[POOL_HALO - w // 2:POOL_HALO - w // 2 + tm, cols]
        for dlt in range(-w // 2 + 1, w // 2):
            acc = acc + ext_ref[POOL_HALO + dlt:POOL_HALO + dlt + tm, cols]
        lo = jnp.maximum(pos - w // 2, 0)
        hi = jnp.minimum(pos - w // 2 + w, slen)
        mean = acc / (hi - lo).astype(F32)
        diff = (mean - ext_ref[POOL_HALO:POOL_HALO + tm, cols]).astype(BF16)
        y_ref[:, cols] = jnp.dot(diff, wp_ref[g], preferred_element_type=F32) * ps_ref[:, cols]

    gate = _mod_row(mod_ref, r, 2, d)
    o_ref[...] = x_ref[...] + gate * _rms(y_ref[...], gpost_ref[...], NORM_EPS)


def _pool_mix(xu, mod, gpre, gpost, wp, ps, *, layer, ic, seq, ctx_len, n_batch, rows):
    d = xu.shape[1]
    n_lat_rows = n_batch * seq
    tm = _tile(256, seq, *([ctx_len] if rows > n_lat_rows else []))
    assert tm % POOL_HALO == 0 and POOL_HALO >= max(POOL_WINDOWS) // 2
    hb = tm // POOL_HALO
    last_hblk = xu.shape[0] // POOL_HALO - 1
    kern = functools.partial(_pool_kernel, tm=tm, seq=seq, ctx_len=ctx_len, n_batch=n_batch,
                             n_lat_rows=n_lat_rows)
    return pl.pallas_call(
        kern,
        out_shape=jax.ShapeDtypeStruct((rows, d), F32),
        grid=(rows // tm,),
        in_specs=[
            pl.BlockSpec((tm, d), lambda i: (i, 0)),
            pl.BlockSpec((POOL_HALO, d), lambda i: (jnp.maximum(i * hb - 1, 0), 0)),
            pl.BlockSpec((POOL_HALO, d), lambda i: (jnp.minimum((i + 1) * hb, last_hblk), 0)),
            _mod_spec(mod, layer, 1),
            _vec_spec(d, layer),
            _vec_spec(d, layer),
            pl.BlockSpec((None,) + wp.shape[1:], lambda i: (ic, 0, 0, 0)),
            _vec_spec(d, ic),
        ],
        out_specs=pl.BlockSpec((tm, d), lambda i: (i, 0)),
        scratch_shapes=[pltpu.VMEM((tm + 2 * POOL_HALO, d), F32), pltpu.VMEM((tm, d), F32)],
        compiler_params=_cparams(1),
        name="pool_mix",
    )(xu, xu, xu, mod, gpre, gpost, wp, ps)


def kernel(x, c, ctx, c_ctx, w_mod, b_mod, g_mix_pre, g_mix_post, g_mlp_pre, g_mlp_post,
           w_qkv, w_attn_out, lambda_q1, lambda_k1, lambda_q2, lambda_k2, g_subln,
           w_fourier_out, w_pool, pool_scale, w_mlp_in, w_mlp_out):
    n_batch, seq, d = x.shape
    ctx_len = ctx.shape[1]
    depth = w_mod.shape[0]
    n_lat_rows = n_batch * seq
    n_ctx_rows = n_batch * ctx_len
    n_rows = n_lat_rows + n_ctx_rows
    n_heads = w_attn_out.shape[1] // V_HEAD_DIM

    n_cond = -(-(n_batch + 1) // 8) * 8
    cond = jnp.concatenate(
        [c, c_ctx[None, :], jnp.zeros((n_cond - n_batch - 1, d), F32)], axis=0)
    b_mod3 = b_mod.reshape(depth, 1, b_mod.shape[1])
    mods = [_mod_layers(cond, w_mod, b_mod3, 1)] + [None] * (depth - 1)

    w_o_b = w_attn_out.astype(BF16)
    w_f_b = w_fourier_out.astype(BF16)
    w_p_b = w_pool.astype(BF16)
    w1_b = w2_b = None

    row3 = lambda t: t.reshape(t.shape[0], 1, t.shape[1])
    g_mix_pre, g_mix_post, g_mlp_pre, g_mlp_post = map(
        row3, (g_mix_pre, g_mix_post, g_mlp_pre, g_mlp_post))
    lambda_q1, lambda_k1, lambda_q2, lambda_k2, g_subln, pool_scale = map(
        row3, (lambda_q1, lambda_k1, lambda_q2, lambda_k2, g_subln, pool_scale))
    lam_params = (lambda_q1, lambda_k1, lambda_q2, lambda_k2)

    rope = _rope_tables(seq)
    common = dict(seq=seq, n_batch=n_batch)

    x_l = x.reshape(n_lat_rows, d)
    x_c = ctx.reshape(n_ctx_rows, d)
    xu = None

    ia = ib = ic = 0
    for i in range(depth):
        last = i == depth - 1
        kind = i % N_MIXERS
        out_rows = n_lat_rows if last else n_rows
        mod = mods[i]
        if xu is None and kind != 0:
            xu = jnp.concatenate([x_l, x_c], axis=0)
        if kind == 0:
            lam_init = 0.8 - 0.6 * math.exp(-0.3 * i)
            qkw = dict(layer=i, ia=ia, **common)
            akw = dict(ia=ia, lam_init=lam_init, n_batch=n_batch)
            if xu is None:
                qkv_l = _qkv_proj(x_l, mod, g_mix_pre, w_qkv, rope, row0=0, **qkw)
                qkv_c = _qkv_proj(x_c, mod, g_mix_pre, w_qkv, rope, row0=n_lat_rows, **qkw)
                ctx_row0 = 0
            else:
                qkv_l = qkv_c = _qkv_proj(xu, mod, g_mix_pre, w_qkv, rope, row0=0, **qkw)
                ctx_row0 = n_lat_rows
            ctx_seg = (qkv_c, ctx_row0, ctx_len)
            cast = (i, [w_mlp_in, w_mlp_out]) if w1_b is None else None
            todo = [l for l in range(depth) if mods[l] is None]
            o_l, cast_out, mod_out = _attention(
                qkv_l, 0, seq, [(qkv_l, 0, seq), ctx_seg], lam_params, g_subln, hb=1, cast=cast,
                mods=(cond, w_mod, b_mod3, todo) if todo else None, **akw)
            if cast is not None:
                w1_b, w2_b = cast_out
            for l, m in zip(todo, mod_out):
                mods[l] = m
            o_c = None if last else _attention(
                qkv_c, ctx_row0, ctx_len, [ctx_seg], lam_params, g_subln, hb=n_heads, **akw)[0]
            if xu is None:
                xu = _proj_res(o_l, o_c, x_l, None if last else x_c, mod, g_mix_post, w_o_b,
                               layer=i, iw=ia, **common)
            else:
                xu = _proj_res(o_l, o_c, xu, None, mod, g_mix_post, w_o_b,
                               layer=i, iw=ia, **common)
            ia += 1
        elif kind == 1:
            cg = d // N_FFT_GROUPS
            cc, sc = _dft_tables(cg)
            cs = jnp.concatenate([cc, sc], axis=1).astype(BF16)
            a, b = _fourier_channels(xu, mod, g_mix_pre, cs, layer=i, rows=out_rows, **common)
            tc, ts = _dft_tables(seq)
            y_l = _fourier_seq(a, b, tc.astype(BF16), (-ts).astype(BF16),
                               n_batch=n_batch, length=seq, row0=0)
            y_c = None
            if not last:
                tcc, tsc = _dft_tables(ctx_len)
                y_c = _fourier_seq(a, b, tcc.astype(BF16), (-tsc).astype(BF16),
                                   n_batch=n_batch, length=ctx_len, row0=n_lat_rows)
            xu = _proj_res(y_l, y_c, xu, None, mod, g_mix_post, w_f_b, layer=i, iw=ib, **common)
            ib += 1
        else:
            xu = _pool_mix(xu, mod, g_mix_pre, g_mix_post, w_p_b, pool_scale, layer=i, ic=ic,
                           ctx_len=ctx_len, rows=out_rows, **common)
            ic += 1
        w_next = None if last else (w_mlp_in, w_mlp_out)
        xu, w1_b, w2_b = _mlp(xu, mod, g_mlp_pre, g_mlp_post, w1_b, w2_b, w_next, layer=i,
                              rows=out_rows, **common)
    return xu[:n_lat_rows].reshape(n_batch, seq, d)
```

```python
import functools
import math

import jax
import jax.numpy as jnp
from jax import lax
from jax.experimental import pallas as pl
from jax.experimental.pallas import tpu as pltpu

F32 = jnp.float32
BF16 = jnp.bfloat16

GRID_W = 64
N_MIXERS = 3
HEAD_DIM = 64
V_HEAD_DIM = 2 * HEAD_DIM
ROPE_BASE = 10000.0
AXIS_ROT = HEAD_DIM // 2
SUBLN_EPS = 1e-5
N_FFT_GROUPS = 4
POOL_WINDOWS = (2, 4, 8, 16)
NORM_EPS = 1e-6
N_MOD = 6
POOL_HALO = 8

LANES = 128
V7X_VMEM_BYTES = 64 * 1024 * 1024
VMEM_LIMIT = V7X_VMEM_BYTES - 4 * 1024 * 1024


def _cparams(n_axes):
    return pltpu.CompilerParams(
        dimension_semantics=("arbitrary",) * n_axes, vmem_limit_bytes=VMEM_LIMIT)


def _tile(pref, *counts):
    t = min(pref, *counts)
    while t > 8 and any(c % t for c in counts):
        t -= 8
    assert t >= 8 and all(c % t == 0 for c in counts), (pref, counts)
    return t


def _rms(x, g, eps):
    return x * lax.rsqrt(jnp.mean(x * x, axis=-1, keepdims=True) + eps) * g


def _mod_row(mod_ref, r, k, d):
    return mod_ref[pl.ds(r, 1), k * d:(k + 1) * d]


def _prenorm(x, g, mod_ref, r, k_shift, d):
    sh = _mod_row(mod_ref, r, k_shift, d)
    sc = _mod_row(mod_ref, r, k_shift + 1, d)
    inv = lax.rsqrt(jnp.mean(x * x, axis=-1, keepdims=True) + NORM_EPS)
    return (x * inv) * (g * (1.0 + sc)) + sh


def _mod_spec(mod, layer, n_axes):
    index = layer if mod.shape[0] > 1 else 0
    return pl.BlockSpec((None,) + mod.shape[1:], lambda *_: (index, 0, 0))


def _vec_spec(width, index):
    return pl.BlockSpec((None, 1, width), lambda *_: (index, 0, 0))


def _mod_kernel(cond_ref, w_ref, b_ref, o_ref):
    s = jax.nn.silu(cond_ref[...]).astype(BF16)
    o_ref[...] = jnp.dot(s, w_ref[...].astype(BF16), preferred_element_type=F32) + b_ref[...]


def _mod_layers(cond, w_mod, b_mod, n_layers):
    depth, d, n = w_mod.shape
    rows = cond.shape[0]
    tn = _tile(1024, n)
    return pl.pallas_call(
        _mod_kernel,
        out_shape=jax.ShapeDtypeStruct((n_layers, rows, n), F32),
        grid=(n_layers, n // tn),
        in_specs=[
            pl.BlockSpec((rows, d), lambda l, j: (0, 0)),
            pl.BlockSpec((None, d, tn), lambda l, j: (l, 0, j)),
            pl.BlockSpec((None, 1, tn), lambda l, j: (l, 0, j)),
        ],
        out_specs=pl.BlockSpec((None, rows, tn), lambda l, j: (l, 0, j)),
        compiler_params=_cparams(2),
        name="ada_mod",
    )(cond, w_mod, b_mod)


def _qkv_kernel(x_ref, mod_ref, g_ref, w_ref, tab_ref, o_ref, u_ref,
                *, tm, tn, seq, n_batch, row0, n_qk_j):
    i = pl.program_id(0)
    j = pl.program_id(1)
    d = x_ref.shape[1]

    def project(use_tables):
        rb = _tile(256, tm)
        w = w_ref[...].astype(BF16)
        for b in range(tm // rb):
            rows = slice(b * rb, (b + 1) * rb)
            y = jnp.dot(u_ref[rows, :], w, preferred_element_type=F32)
            if use_tables:
                cos = tab_ref[0, rows, :]
                sa = tab_ref[1, rows, :]
                sb = tab_ref[2, rows, :]
            for c in range(tn // LANES):
                yc = y[:, c * LANES:(c + 1) * LANES]
                if use_tables:
                    yc = (yc * cos + pltpu.roll(yc, LANES - AXIS_ROT // 2, 1) * sa
                          + pltpu.roll(yc, AXIS_ROT // 2, 1) * sb)
                o_ref[c, rows, :] = yc.astype(BF16)

    @pl.when(j == 0)
    def _():
        r = jnp.minimum(lax.div(row0 + i * tm, seq), n_batch)
        u_ref[...] = _prenorm(x_ref[...], g_ref[...], mod_ref, r, 0, d).astype(BF16)
        project(True)

    @pl.when(jnp.logical_and(j > 0, j < n_qk_j))
    def _():
        project(True)

    @pl.when(j >= n_qk_j)
    def _():
        project(False)


def _qkv_proj(xa, mod, g, w, rope, *, layer, ia, seq, n_batch, row0):
    rows, d = xa.shape
    n = w.shape[2]
    n_lat_rows = n_batch * seq
    n_lat_here = max(0, min(rows, n_lat_rows - row0))
    tm = _tile(1024, seq, *[c for c in (n_lat_here, rows - n_lat_here) if c])
    tn = _tile(512, n // 3)
    assert tn % LANES == 0 and row0 % tm == 0
    lat_tiles_per_seq = seq // tm
    n_lat_tiles = n_lat_here // tm
    n_q_j = n // 3 // tn

    def tab_index(i, j):
        lat = jnp.where(j < n_q_j, 0, jnp.where(j < 2 * n_q_j, 1, 2))
        ctx = jnp.where(j < n_q_j, 3, 2)
        return (jnp.where(i < n_lat_tiles, lat, ctx), 0, (row0 // tm + i) % lat_tiles_per_seq, 0)

    kern = functools.partial(_qkv_kernel, tm=tm, tn=tn, seq=seq, n_batch=n_batch, row0=row0,
                             n_qk_j=2 * n_q_j)
    return pl.pallas_call(
        kern,
        out_shape=jax.ShapeDtypeStruct((n // LANES, rows, LANES), BF16),
        grid=(rows // tm, n // tn),
        in_specs=[
            pl.BlockSpec((tm, d), lambda i, j: (i, 0)),
            _mod_spec(mod, layer, 2),
            _vec_spec(d, layer),
            pl.BlockSpec((None, d, tn), lambda i, j: (ia, 0, j)),
            pl.BlockSpec((None, 3, tm, LANES), tab_index),
        ],
        out_specs=pl.BlockSpec((tn // LANES, tm, LANES), lambda i, j: (j, i, 0)),
        scratch_shapes=[pltpu.VMEM((tm, d), BF16)],
        compiler_params=_cparams(2),
        name="qkv_proj",
    )(xa, mod, g, w, rope)


def _rope_tables(seq):
    rows = seq // GRID_W
    row = jnp.repeat(jnp.arange(rows), GRID_W).astype(F32)
    col = jnp.tile(jnp.arange(GRID_W), rows).astype(F32)
    n_freq = AXIS_ROT // 2
    inv = 1.0 / (ROPE_BASE ** (jnp.arange(n_freq, dtype=F32) / n_freq))
    ang_r = row[:, None] * inv
    ang_c = col[:, None] * inv
    cr, sr, cc, sc = jnp.cos(ang_r), jnp.sin(ang_r), jnp.cos(ang_c), jnp.sin(ang_c)
    z = jnp.zeros_like(sr)
    reps = LANES // HEAD_DIM
    cos = jnp.tile(jnp.concatenate([cr, cr, cc, cc], axis=1), (1, reps))
    sa = jnp.tile(jnp.concatenate([-sr, z, -sc, z], axis=1), (1, reps))
    sb = jnp.tile(jnp.concatenate([z, sr, z, sc], axis=1), (1, reps))
    rot = jnp.stack([cos, sa, sb])
    ident = jnp.stack([jnp.ones_like(cos), jnp.zeros_like(cos), jnp.zeros_like(cos)])
    qs = math.log2(math.e) / math.sqrt(HEAD_DIM)
    return jnp.stack([rot * qs, rot, ident, ident * qs])


def _lambda(lq1_ref, lk1_ref, lq2_ref, lk2_ref, lam_init):
    return (jnp.exp(jnp.sum(lq1_ref[...] * lk1_ref[...], axis=-1, keepdims=True))
            - jnp.exp(jnp.sum(lq2_ref[...] * lk2_ref[...], axis=-1, keepdims=True)) + lam_init)


def _attn_kernel(*refs, n_seg, lam_init, rb, n_cast, n_mod):
    refs = list(refs)
    take = lambda n: [refs.pop(0) for _ in range(n)]
    (q_ref,) = take(1)
    seg_refs = take(2 * n_seg)
    lq1_ref, lk1_ref, lq2_ref, lk2_ref, gs_ref = take(5)
    cast_in = take(n_cast)
    cond_ref, = take(1) if n_mod else (None,)
    wm_refs, bm_refs = take(n_mod), take(n_mod)
    (o_ref,) = take(1)
    cast_out, mod_out = take(n_cast), take(n_mod)
    kk_ref, va_ref = refs

    for w_in, w_out in zip(cast_in, cast_out):
        w_out[...] = w_in[...].astype(BF16)
    if n_mod:
        sc = jax.nn.silu(cond_ref[...]).astype(BF16)
        for wm_ref, bm_ref, mo_ref in zip(wm_refs, bm_refs, mod_out):
            mo_ref[...] = (jnp.dot(sc, wm_ref[...].astype(BF16), preferred_element_type=F32)
                           + bm_ref[...])

    hb, q_len, _ = q_ref.shape
    n_blocks = q_len // rb
    lam = _lambda(lq1_ref, lk1_ref, lq2_ref, lk2_ref, lam_init)
    gs = gs_ref[...]
    lane = lax.broadcasted_iota(jnp.int32, (rb, LANES), 1)
    va_ref[:, V_HEAD_DIM:] = jnp.ones((va_ref.shape[0], V_HEAD_DIM), BF16)

    for h in range(hb):
        off = 0
        for s in range(n_seg):
            n = seg_refs[2 * s].shape[1]
            kk_ref[off:off + n, :] = seg_refs[2 * s][h]
            va_ref[off:off + n, 0:V_HEAD_DIM] = seg_refs[2 * s + 1][h]
            off += n

        def scores(r):
            q = q_ref[h, r * rb:(r + 1) * rb, :]
            qq = jnp.concatenate([jnp.where(lane < HEAD_DIM, q, jnp.zeros_like(q)),
                                  jnp.where(lane >= HEAD_DIM, q, jnp.zeros_like(q))], axis=0)
            return lax.dot_general(qq, kk_ref[...], (((1,), (1,)), ((), ())),
                                   preferred_element_type=F32)

        def finish(r, s):
            p = jnp.exp2(s - jnp.max(s, axis=-1, keepdims=True)).astype(BF16)
            acc = jnp.dot(p, va_ref[...], preferred_element_type=F32)
            ratio = acc[:, :V_HEAD_DIM] / acc[:, V_HEAD_DIM:]
            o = ratio[:rb] - lam * ratio[rb:]
            o = _rms(o, gs, SUBLN_EPS) * (1.0 - lam_init)
            o_ref[r * rb:(r + 1) * rb, h * V_HEAD_DIM:(h + 1) * V_HEAD_DIM] = o.astype(BF16)

        s_next = scores(0)
        for r in range(n_blocks):
            s_cur = s_next
            if r + 1 < n_blocks:
                s_next = scores(r + 1)
            finish(r, s_cur)


def _attention(q_arr, q_row0, q_len, segs, lam_params, gs, *, ia, lam_init, n_batch, hb,
               cast=None, mods=None):
    n_heads = q_arr.shape[0] // 3
    assert n_heads % hb == 0 and q_row0 % q_len == 0
    nk = sum(n for _, _, n in segs)
    q_blk0 = q_row0 // q_len
    in_specs = [pl.BlockSpec((hb, q_len, LANES), lambda b, h: (h, q_blk0 + b, 0))]
    args = [q_arr]
    for arr, row0, n in segs:
        assert row0 % n == 0 and arr.shape[0] == 3 * n_heads
        for part in (1, 2):
            in_specs.append(pl.BlockSpec(
                (hb, n, LANES),
                lambda b, h, blk0=row0 // n, col0=part * n_heads // hb: (col0 + h, blk0 + b, 0)))
            args.append(arr)
    in_specs += [_vec_spec(HEAD_DIM, ia)] * 4
    in_specs.append(_vec_spec(V_HEAD_DIM, ia))
    args += [*lam_params, gs]
    out_shape = [jax.ShapeDtypeStruct((n_batch * q_len, n_heads * V_HEAD_DIM), BF16)]
    out_specs = [pl.BlockSpec((q_len, hb * V_HEAD_DIM), lambda b, h: (b, h))]
    n_hg = n_heads // hb
    n_steps = n_batch * n_hg
    step = lambda b, h: b * n_hg + h
    n_cast = n_mod = 0
    if cast is not None:
        layer, stacks = cast
        n_cast = len(stacks)
        for w in stacks:
            _, rows, cols = w.shape
            assert rows % n_steps == 0
            rblk = rows // n_steps
            in_specs.append(pl.BlockSpec((None, rblk, cols),
                                         lambda b, h, layer=layer: (layer, step(b, h), 0)))
            args.append(w)
            out_shape.append(jax.ShapeDtypeStruct((rows, cols), BF16))
            out_specs.append(pl.BlockSpec((rblk, cols), lambda b, h: (step(b, h), 0)))
    if mods is not None:
        cond, w_mod, b_mod3, layers = mods
        n_mod = len(layers)
        _, dm, nm = w_mod.shape
        tnm = min(t for t in range(LANES, nm + 1, LANES) if nm % t == 0 and nm // t <= n_steps)
        blk = lambda b, h: jnp.minimum(step(b, h), nm // tnm - 1)
        in_specs.append(pl.BlockSpec(cond.shape, lambda b, h: (0, 0)))
        args.append(cond)
        for l in layers:
            in_specs.append(pl.BlockSpec((None, dm, tnm), lambda b, h, l=l: (l, 0, blk(b, h))))
            args.append(w_mod)
        for l in layers:
            in_specs.append(pl.BlockSpec((None, 1, tnm), lambda b, h, l=l: (l, 0, blk(b, h))))
            args.append(b_mod3)
            out_shape.append(jax.ShapeDtypeStruct((1, cond.shape[0], nm), F32))
            out_specs.append(pl.BlockSpec((None, cond.shape[0], tnm),
                                          lambda b, h: (0, 0, blk(b, h))))
    outs = pl.pallas_call(
        functools.partial(_attn_kernel, n_seg=len(segs), lam_init=lam_init,
                          rb=_tile(128, q_len), n_cast=n_cast, n_mod=n_mod),
        out_shape=out_shape,
        grid=(n_batch, n_hg),
        in_specs=in_specs,
        out_specs=out_specs,
        scratch_shapes=[pltpu.VMEM((nk, LANES), BF16), pltpu.VMEM((nk, 2 * V_HEAD_DIM), BF16)],
        compiler_params=_cparams(2),
        name="diff_attn",
    )(*args)
    return outs[0], outs[1:1 + n_cast], outs[1 + n_cast:]


def _proj_res_kernel(*refs, tm, seq, n_batch, n_lat_tiles, has_ctx, split_x):
    refs = list(refs)
    inl_ref = refs.pop(0)
    inc_ref = refs.pop(0) if has_ctx else None
    xl_ref = refs.pop(0)
    xc_ref = refs.pop(0) if split_x else xl_ref
    mod_ref, g_ref, w_ref, o_ref = refs
    i = pl.program_id(0)
    d = o_ref.shape[1]

    def body(inp_ref, x_ref):
        r = jnp.minimum(lax.div(i * tm, seq), n_batch)
        gg = g_ref[...] * _mod_row(mod_ref, r, 2, d)
        rb = _tile(256, tm)
        for b in range(tm // rb):
            rows = slice(b * rb, (b + 1) * rb)
            y = jnp.dot(inp_ref[rows, :], w_ref[...], preferred_element_type=F32)
            inv = lax.rsqrt(jnp.mean(y * y, axis=-1, keepdims=True) + NORM_EPS)
            o_ref[rows, :] = x_ref[rows, :] + (y * inv) * gg

    if has_ctx:
        @pl.when(i < n_lat_tiles)
        def _():
            body(inl_ref, xl_ref)

        @pl.when(i >= n_lat_tiles)
        def _():
            body(inc_ref, xc_ref)
    else:
        body(inl_ref, xl_ref)


def _proj_res(inp_l, inp_c, x_l, x_c, mod, g, w, *, layer, iw, seq, n_batch):
    d = x_l.shape[1]
    n_lat_rows = inp_l.shape[0]
    has_ctx = inp_c is not None
    split_x = x_c is not None
    assert has_ctx or not split_x
    rows = n_lat_rows + (inp_c.shape[0] if has_ctx else 0)
    tm = _tile(512, seq, *([inp_c.shape[0]] if has_ctx else []))
    n_lat_tiles = n_lat_rows // tm
    kin = inp_l.shape[1]
    lat_index = lambda i: (jnp.minimum(i, n_lat_tiles - 1), 0)
    ctx_index = lambda i: (jnp.maximum(i - n_lat_tiles, 0), 0)
    in_specs = [pl.BlockSpec((tm, kin), lat_index)]
    args = [inp_l]
    if has_ctx:
        in_specs.append(pl.BlockSpec((tm, kin), ctx_index))
        args.append(inp_c)
    if split_x:
        in_specs += [pl.BlockSpec((tm, d), lat_index), pl.BlockSpec((tm, d), ctx_index)]
        args += [x_l, x_c]
    else:
        in_specs.append(pl.BlockSpec((tm, d), lambda i: (i, 0)))
        args.append(x_l)
    in_specs += [
        _mod_spec(mod, layer, 1),
        _vec_spec(d, layer),
        pl.BlockSpec((None,) + w.shape[1:], lambda i: (iw, 0, 0)),
    ]
    args += [mod, g, w]
    kern = functools.partial(_proj_res_kernel, tm=tm, seq=seq, n_batch=n_batch,
                             n_lat_tiles=n_lat_tiles, has_ctx=has_ctx, split_x=split_x)
    return pl.pallas_call(
        kern,
        out_shape=jax.ShapeDtypeStruct((rows, d), F32),
        grid=(rows // tm,),
        in_specs=in_specs,
        out_specs=pl.BlockSpec((tm, d), lambda i: (i, 0)),
        compiler_params=_cparams(1),
        name="proj_res",
    )(*args)


def _mlp_kernel(*refs, tm, rb, seq, n_batch, cast_next, single_chunk):
    if cast_next:
        (x_ref, mod_ref, gpre_ref, gpost_ref, w1_ref, w2_ref, w1n_ref, w2n_ref,
         o_ref, w1o_ref, w2o_ref, u_ref) = refs
    else:
        x_ref, mod_ref, gpre_ref, gpost_ref, w1_ref, w2_ref, o_ref, u_ref = refs
    i = pl.program_id(0)
    j = pl.program_id(1)
    d = x_ref.shape[1]
    r = jnp.minimum(lax.div(i * tm, seq), n_batch)

    @pl.when(j == 0)
    def _():
        u_ref[...] = _prenorm(x_ref[...], gpre_ref[...], mod_ref, r, 3, d).astype(BF16)

    def hidden(b):
        h = jnp.dot(u_ref[b * rb:(b + 1) * rb, :], w1_ref[...], preferred_element_type=F32)
        h = jnp.maximum(h, 0.0)
        return (h * h).astype(BF16)

    def chunk(first, last):
        n_blocks = tm // rb
        if last:
            gg = gpost_ref[...] * _mod_row(mod_ref, r, 5, d)
        h_next = hidden(0)
        for b in range(n_blocks):
            h_cur = h_next
            if b + 1 < n_blocks:
                h_next = hidden(b + 1)
            rows = slice(b * rb, (b + 1) * rb)
            y = jnp.dot(h_cur, w2_ref[...], preferred_element_type=F32)
            if not first:
                y = o_ref[rows, :] + y
            if last:
                inv = lax.rsqrt(jnp.mean(y * y, axis=-1, keepdims=True) + NORM_EPS)
                y = x_ref[rows, :] + (y * inv) * gg
            o_ref[rows, :] = y
        if cast_next:
            w1o_ref[...] = w1n_ref[...].astype(BF16)
            w2o_ref[...] = w2n_ref[...].astype(BF16)

    n_j = pl.num_programs(1)

    @pl.when(j == 0)
    def _():
        chunk(True, single_chunk)

    if not single_chunk:
        @pl.when(jnp.logical_and(j > 0, j < n_j - 1))
        def _():
            chunk(False, False)

        @pl.when(j == n_j - 1)
        def _():
            chunk(False, True)


CAST_BLOCK_BYTES = 1024 * 1024


def _mlp(xu, mod, gpre, gpost, w1, w2, w_next, *, layer, seq, n_batch, rows):
    d = xu.shape[1]
    dff = w1.shape[1]
    tm = _tile(1024, seq, *([xu.shape[0] - n_batch * seq] if rows > n_batch * seq else []))
    tf = _tile(512, dff)
    n_j = dff // tf
    n_steps = (rows // tm) * n_j
    cast_next = w_next is not None
    kern = functools.partial(_mlp_kernel, tm=tm, rb=_tile(512, tm), seq=seq, n_batch=n_batch,
                             cast_next=cast_next, single_chunk=n_j == 1)
    in_specs = [
        pl.BlockSpec((tm, d), lambda i, j: (i, 0)),
        _mod_spec(mod, layer, 2),
        _vec_spec(d, layer),
        _vec_spec(d, layer),
        pl.BlockSpec((d, tf), lambda i, j: (0, j)),
        pl.BlockSpec((tf, d), lambda i, j: (j, 0)),
    ]
    args = [xu, mod, gpre, gpost, w1, w2]
    out_shape = [jax.ShapeDtypeStruct((rows, d), F32)]
    out_specs = [pl.BlockSpec((tm, d), lambda i, j: (i, 0))]
    if cast_next:
        r1 = _tile(CAST_BLOCK_BYTES // (4 * dff), d)
        r2 = _tile(CAST_BLOCK_BYTES // (4 * d), dff)
        c1, c2 = d // r1, dff // r2
        assert c1 + c2 <= n_steps

        def blk1(i, j):
            return jnp.minimum(i * n_j + j, c1 - 1)

        def blk2(i, j):
            return jnp.clip(i * n_j + j - c1, 0, c2 - 1)

        in_specs += [pl.BlockSpec((None, r1, dff), lambda i, j: (layer + 1, blk1(i, j), 0)),
                     pl.BlockSpec((None, r2, d), lambda i, j: (layer + 1, blk2(i, j), 0))]
        args += list(w_next)
        out_shape += [jax.ShapeDtypeStruct((d, dff), BF16), jax.ShapeDtypeStruct((dff, d), BF16)]
        out_specs += [pl.BlockSpec((r1, dff), lambda i, j: (blk1(i, j), 0)),
                      pl.BlockSpec((r2, d), lambda i, j: (blk2(i, j), 0))]
    outs = pl.pallas_call(
        kern,
        out_shape=out_shape,
        grid=(rows // tm, n_j),
        in_specs=in_specs,
        out_specs=out_specs,
        scratch_shapes=[pltpu.VMEM((tm, d), BF16)],
        compiler_params=_cparams(2),
        name="sq_relu_mlp",
    )(*args)
    return outs if cast_next else (outs[0], None, None)


def _dft_tables(n):
    b = max(f for f in range(1, math.isqrt(n) + 1) if n % f == 0)
    a = n // b
    j = jnp.arange(n, dtype=jnp.int32)[:, None]
    ang1 = ((j * jnp.arange(a, dtype=jnp.int32)[None, :]) % a).astype(F32) * (2.0 * math.pi / a)
    ang0 = ((j * jnp.arange(b, dtype=jnp.int32)[None, :]) % n).astype(F32) * (2.0 * math.pi / n)
    c1, s1 = jnp.cos(ang1)[:, :, None], jnp.sin(ang1)[:, :, None]
    c0, s0 = jnp.cos(ang0)[:, None, :], jnp.sin(ang0)[:, None, :]
    scale = 1.0 / math.sqrt(n)
    cos = (c1 * c0 - s1 * s0).reshape(n, n) * scale
    sin = (s1 * c0 + c1 * s0).reshape(n, n) * scale
    return cos, sin


def _fourier_ch_kernel(x_ref, mod_ref, g_ref, cs_ref, a_ref, b_ref, *, tm, seq, n_batch):
    i = pl.program_id(0)
    d = x_ref.shape[1]
    cg = d // N_FFT_GROUPS
    r = jnp.minimum(lax.div(i * tm, seq), n_batch)
    u = _prenorm(x_ref[...], g_ref[...], mod_ref, r, 0, d).astype(BF16)
    for g in range(N_FFT_GROUPS):
        ab = jnp.dot(u[:, g * cg:(g + 1) * cg], cs_ref[...], preferred_element_type=F32)
        a_ref[:, g * cg:(g + 1) * cg] = ab[:, :cg].astype(BF16)
        b_ref[:, g * cg:(g + 1) * cg] = ab[:, cg:].astype(BF16)


def _fourier_channels(xu, mod, g, cs, *, layer, seq, n_batch, rows):
    d = xu.shape[1]
    tm = _tile(512, seq, *([xu.shape[0] - n_batch * seq] if rows > n_batch * seq else []))
    kern = functools.partial(_fourier_ch_kernel, tm=tm, seq=seq, n_batch=n_batch)
    return pl.pallas_call(
        kern,
        out_shape=[jax.ShapeDtypeStruct((rows, d), BF16)] * 2,
        grid=(rows // tm,),
        in_specs=[
            pl.BlockSpec((tm, d), lambda i: (i, 0)),
            _mod_spec(mod, layer, 1),
            _vec_spec(d, layer),
            pl.BlockSpec(cs.shape, lambda i: (0, 0)),
        ],
        out_specs=[pl.BlockSpec((tm, d), lambda i: (i, 0))] * 2,
        compiler_params=_cparams(1),
        name="fourier_channels",
    )(xu, mod, g, cs)


def _fourier_seq_kernel(tc_ref, tms_ref, a_ref, b_ref, o_ref, *, tmm):
    n = tc_ref.shape[0]
    for mi in range(n // tmm):
        rows = slice(mi * tmm, (mi + 1) * tmm)
        y = (jnp.dot(tc_ref[rows, :], a_ref[...], preferred_element_type=F32)
             + jnp.dot(tms_ref[rows, :], b_ref[...], preferred_element_type=F32))
        o_ref[rows, :] = y.astype(BF16)


def _fourier_seq(a, b, tc, tms, *, n_batch, length, row0):
    d = a.shape[1]
    tn = _tile(512, d)
    tmm = _tile(512, length)
    blk0 = row0 // length
    assert row0 % length == 0
    ab_spec = pl.BlockSpec((length, tn), lambda bi, ni: (blk0 + bi, ni))
    tab_spec = pl.BlockSpec((length, length), lambda bi, ni: (0, 0))
    return pl.pallas_call(
        functools.partial(_fourier_seq_kernel, tmm=tmm),
        out_shape=jax.ShapeDtypeStruct((n_batch * length, d), BF16),
        grid=(n_batch, d // tn),
        in_specs=[tab_spec, tab_spec, ab_spec, ab_spec],
        out_specs=pl.BlockSpec((length, tn), lambda bi, ni: (bi, ni)),
        compiler_params=_cparams(2),
        name="fourier_seq",
    )(tc, tms, a, b)


def _pool_kernel(x_ref, xp_ref, xn_ref, mod_ref, gpre_ref, gpost_ref, wp_ref, ps_ref, o_ref,
                 ext_ref, y_ref, *, tm, seq, ctx_len, n_batch, n_lat_rows):
    i = pl.program_id(0)
    d = x_ref.shape[1]
    cg = d // len(POOL_WINDOWS)
    row0 = i * tm
    is_lat = row0 < n_lat_rows
    slen = jnp.where(is_lat, seq, ctx_len)
    pos0 = jnp.where(is_lat, lax.rem(row0, seq), lax.rem(row0 - n_lat_rows, ctx_len))
    r = jnp.minimum(lax.div(row0, seq), n_batch)
    g_pre = gpre_ref[...]
    keep_prev = jnp.where(pos0 > 0, 1.0, 0.0).astype(F32)
    keep_next = jnp.where(pos0 + tm < slen, 1.0, 0.0).astype(F32)
    ext_ref[0:POOL_HALO, :] = _prenorm(xp_ref[...], g_pre, mod_ref, r, 0, d) * keep_prev
    ext_ref[POOL_HALO:POOL_HALO + tm, :] = _prenorm(x_ref[...], g_pre, mod_ref, r, 0, d)
    ext_ref[POOL_HALO + tm:, :] = _prenorm(xn_ref[...], g_pre, mod_ref, r, 0, d) * keep_next

    pos = pos0 + lax.broadcasted_iota(jnp.int32, (tm, 1), 0)
    for g, w in enumerate(POOL_WINDOWS):
        cols = slice(g * cg, (g + 1) * cg)
        acc = ext_ref[POOL_HALO - w // 2:POOL_HALO - w // 2 + tm, cols]
        for dlt in range(-w // 2 + 1, w // 2):
            acc = acc + ext_ref[POOL_HALO + dlt:POOL_HALO + dlt + tm, cols]
        lo = jnp.maximum(pos - w // 2, 0)
        hi = jnp.minimum(pos - w // 2 + w, slen)
        mean = acc / (hi - lo).astype(F32)
        diff = (mean - ext_ref[POOL_HALO:POOL_HALO + tm, cols]).astype(BF16)
        y_ref[:, cols] = jnp.dot(diff, wp_ref[g], preferred_element_type=F32) * ps_ref[:, cols]

    gate = _mod_row(mod_ref, r, 2, d)
    o_ref[...] = x_ref[...] + gate * _rms(y_ref[...], gpost_ref[...], NORM_EPS)


def _pool_mix(xu, mod, gpre, gpost, wp, ps, *, layer, ic, seq, ctx_len, n_batch, rows):
    d = xu.shape[1]
    n_lat_rows = n_batch * seq
    tm = _tile(256, seq, *([ctx_len] if rows > n_lat_rows else []))
    assert tm % POOL_HALO == 0 and POOL_HALO >= max(POOL_WINDOWS) // 2
    hb = tm // POOL_HALO
    last_hblk = xu.shape[0] // POOL_HALO - 1
    kern = functools.partial(_pool_kernel, tm=tm, seq=seq, ctx_len=ctx_len, n_batch=n_batch,
                             n_lat_rows=n_lat_rows)
    return pl.pallas_call(
        kern,
        out_shape=jax.ShapeDtypeStruct((rows, d), F32),
        grid=(rows // tm,),
        in_specs=[
            pl.BlockSpec((tm, d), lambda i: (i, 0)),
            pl.BlockSpec((POOL_HALO, d), lambda i: (jnp.maximum(i * hb - 1, 0), 0)),
            pl.BlockSpec((POOL_HALO, d), lambda i: (jnp.minimum((i + 1) * hb, last_hblk), 0)),
            _mod_spec(mod, layer, 1),
            _vec_spec(d, layer),
            _vec_spec(d, layer),
            pl.BlockSpec((None,) + wp.shape[1:], lambda i: (ic, 0, 0, 0)),
            _vec_spec(d, ic),
        ],
        out_specs=pl.BlockSpec((tm, d), lambda i: (i, 0)),
        scratch_shapes=[pltpu.VMEM((tm + 2 * POOL_HALO, d), F32), pltpu.VMEM((tm, d), F32)],
        compiler_params=_cparams(1),
        name="pool_mix",
    )(xu, xu, xu, mod, gpre, gpost, wp, ps)


def kernel(x, c, ctx, c_ctx, w_mod, b_mod, g_mix_pre, g_mix_post, g_mlp_pre, g_mlp_post,
           w_qkv, w_attn_out, lambda_q1, lambda_k1, lambda_q2, lambda_k2, g_subln,
           w_fourier_out, w_pool, pool_scale, w_mlp_in, w_mlp_out):
    n_batch, seq, d = x.shape
    ctx_len = ctx.shape[1]
    depth = w_mod.shape[0]
    n_lat_rows = n_batch * seq
    n_ctx_rows = n_batch * ctx_len
    n_rows = n_lat_rows + n_ctx_rows
    n_heads = w_attn_out.shape[1] // V_HEAD_DIM

    n_cond = -(-(n_batch + 1) // 8) * 8
    cond = jnp.concatenate(
        [c, c_ctx[None, :], jnp.zeros((n_cond - n_batch - 1, d), F32)], axis=0)
    b_mod3 = b_mod.reshape(depth, 1, b_mod.shape[1])
    mods = [_mod_layers(cond, w_mod, b_mod3, 1)] + [None] * (depth - 1)

    w_o_b = w_attn_out.astype(BF16)
    w_f_b = w_fourier_out.astype(BF16)
    w_p_b = w_pool.astype(BF16)
    w1_b = w2_b = None

    row3 = lambda t: t.reshape(t.shape[0], 1, t.shape[1])
    g_mix_pre, g_mix_post, g_mlp_pre, g_mlp_post = map(
        row3, (g_mix_pre, g_mix_post, g_mlp_pre, g_mlp_post))
    lambda_q1, lambda_k1, lambda_q2, lambda_k2, g_subln, pool_scale = map(
        row3, (lambda_q1, lambda_k1, lambda_q2, lambda_k2, g_subln, pool_scale))
    lam_params = (lambda_q1, lambda_k1, lambda_q2, lambda_k2)

    rope = _rope_tables(seq)
    common = dict(seq=seq, n_batch=n_batch)

    x_l = x.reshape(n_lat_rows, d)
    x_c = ctx.reshape(n_ctx_rows, d)
    xu = None

    ia = ib = ic = 0
    for i in range(depth):
        last = i == depth - 1
        kind = i % N_MIXERS
        out_rows = n_lat_rows if last else n_rows
        mod = mods[i]
        if xu is None and kind != 0:
            xu = jnp.concatenate([x_l, x_c], axis=0)
        if kind == 0:
            lam_init = 0.8 - 0.6 * math.exp(-0.3 * i)
            qkw = dict(layer=i, ia=ia, **common)
            akw = dict(ia=ia, lam_init=lam_init, n_batch=n_batch)
            if xu is None:
                qkv_l = _qkv_proj(x_l, mod, g_mix_pre, w_qkv, rope, row0=0, **qkw)
                qkv_c = _qkv_proj(x_c, mod, g_mix_pre, w_qkv, rope, row0=n_lat_rows, **qkw)
                ctx_row0 = 0
            else:
                qkv_l = qkv_c = _qkv_proj(xu, mod, g_mix_pre, w_qkv, rope, row0=0, **qkw)
                ctx_row0 = n_lat_rows
            ctx_seg = (qkv_c, ctx_row0, ctx_len)
            cast = (i, [w_mlp_in, w_mlp_out]) if w1_b is None else None
            todo = [l for l in range(depth) if mods[l] is None]
            o_l, cast_out, mod_out = _attention(
                qkv_l, 0, seq, [(qkv_l, 0, seq), ctx_seg], lam_params, g_subln, hb=1, cast=cast,
                mods=(cond, w_mod, b_mod3, todo) if todo else None, **akw)
            if cast is not None:
                w1_b, w2_b = cast_out
            for l, m in zip(todo, mod_out):
                mods[l] = m
            o_c = None if last else _attention(
                qkv_c, ctx_row0, ctx_len, [ctx_seg], lam_params, g_subln, hb=n_heads, **akw)[0]
            if xu is None:
                xu = _proj_res(o_l, o_c, x_l, None if last else x_c, mod, g_mix_post, w_o_b,
                               layer=i, iw=ia, **common)
            else:
                xu = _proj_res(o_l, o_c, xu, None, mod, g_mix_post, w_o_b,
                               layer=i, iw=ia, **common)
            ia += 1
        elif kind == 1:
            cg = d // N_FFT_GROUPS
            cc, sc = _dft_tables(cg)
            cs = jnp.concatenate([cc, sc], axis=1).astype(BF16)
            a, b = _fourier_channels(xu, mod, g_mix_pre, cs, layer=i, rows=out_rows, **common)
            tc, ts = _dft_tables(seq)
            y_l = _fourier_seq(a, b, tc.astype(BF16), (-ts).astype(BF16),
                               n_batch=n_batch, length=seq, row0=0)
            y_c = None
            if not last:
                tcc, tsc = _dft_tables(ctx_len)
                y_c = _fourier_seq(a, b, tcc.astype(BF16), (-tsc).astype(BF16),
                                   n_batch=n_batch, length=ctx_len, row0=n_lat_rows)
            xu = _proj_res(y_l, y_c, xu, None, mod, g_mix_post, w_f_b, layer=i, iw=ib, **common)
            ib += 1
        else:
            xu = _pool_mix(xu, mod, g_mix_pre, g_mix_post, w_p_b, pool_scale, layer=i, ic=ic,
                           ctx_len=ctx_len, rows=out_rows, **common)
            ic += 1
        w_next = None if last else (w_mlp_in, w_mlp_out)
        xu, w1_b, w2_b = _mlp(xu, mod, g_mlp_pre, g_mlp_post, w1_b, w2_b, w_next, layer=i,
                              rows=out_rows, **common)
    return xu[:n_lat_rows].reshape(n_batch, seq, d)
```

```python
import functools
import math

import jax
import jax.numpy as jnp
from jax import lax
from jax.experimental import pallas as pl
from jax.experimental.pallas import tpu as pltpu

F32 = jnp.float32
BF16 = jnp.bfloat16

GRID_W = 64
N_MIXERS = 3
HEAD_DIM = 64
V_HEAD_DIM = 2 * HEAD_DIM
ROPE_BASE = 10000.0
AXIS_ROT = HEAD_DIM // 2
SUBLN_EPS = 1e-5
N_FFT_GROUPS = 4
POOL_WINDOWS = (2, 4, 8, 16)
NORM_EPS = 1e-6
N_MOD = 6
POOL_HALO = 8

LANES = 128
V7X_VMEM_BYTES = 64 * 1024 * 1024
VMEM_LIMIT = V7X_VMEM_BYTES - 4 * 1024 * 1024


def _cparams(n_axes):
    return pltpu.CompilerParams(
        dimension_semantics=("arbitrary",) * n_axes, vmem_limit_bytes=VMEM_LIMIT)


def _tile(pref, *counts):
    t = min(pref, *counts)
    while t > 8 and any(c % t for c in counts):
        t -= 8
    assert t >= 8 and all(c % t == 0 for c in counts), (pref, counts)
    return t


def _rms(x, g, eps):
    return x * lax.rsqrt(jnp.mean(x * x, axis=-1, keepdims=True) + eps) * g


def _mod_row(mod_ref, r, k, d):
    return mod_ref[pl.ds(r, 1), k * d:(k + 1) * d]


def _prenorm(x, g, mod_ref, r, k_shift, d):
    sh = _mod_row(mod_ref, r, k_shift, d)
    sc = _mod_row(mod_ref, r, k_shift + 1, d)
    inv = lax.rsqrt(jnp.mean(x * x, axis=-1, keepdims=True) + NORM_EPS)
    return (x * inv) * (g * (1.0 + sc)) + sh


def _mod_spec(mod, layer, n_axes):
    index = layer if mod.shape[0] > 1 else 0
    return pl.BlockSpec((None,) + mod.shape[1:], lambda *_: (index, 0, 0))


def _vec_spec(width, index):
    return pl.BlockSpec((None, 1, width), lambda *_: (index, 0, 0))


def _mod_kernel(cond_ref, w_ref, b_ref, o_ref):
    s = jax.nn.silu(cond_ref[...]).astype(BF16)
    o_ref[...] = jnp.dot(s, w_ref[...].astype(BF16), preferred_element_type=F32) + b_ref[...]


def _mod_layers(cond, w_mod, b_mod, n_layers):
    depth, d, n = w_mod.shape
    rows = cond.shape[0]
    tn = _tile(1024, n)
    return pl.pallas_call(
        _mod_kernel,
        out_shape=jax.ShapeDtypeStruct((n_layers, rows, n), F32),
        grid=(n_layers, n // tn),
        in_specs=[
            pl.BlockSpec((rows, d), lambda l, j: (0, 0)),
            pl.BlockSpec((None, d, tn), lambda l, j: (l, 0, j)),
            pl.BlockSpec((None, 1, tn), lambda l, j: (l, 0, j)),
        ],
        out_specs=pl.BlockSpec((None, rows, tn), lambda l, j: (l, 0, j)),
        compiler_params=_cparams(2),
        name="ada_mod",
    )(cond, w_mod, b_mod)


def _qkv_kernel(x_ref, mod_ref, g_ref, w_ref, tab_ref, o_ref, u_ref,
                *, tm, tn, seq, n_batch, row0, n_qk_j):
    i = pl.program_id(0)
    j = pl.program_id(1)
    d = x_ref.shape[1]

    def project(use_tables):
        rb = _tile(256, tm)
        w = w_ref[...].astype(BF16)
        for b in range(tm // rb):
            rows = slice(b * rb, (b + 1) * rb)
            y = jnp.dot(u_ref[rows, :], w, preferred_element_type=F32)
            if use_tables:
                cos = tab_ref[0, rows, :]
                sa = tab_ref[1, rows, :]
                sb = tab_ref[2, rows, :]
            for c in range(tn // LANES):
                yc = y[:, c * LANES:(c + 1) * LANES]
                if use_tables:
                    yc = (yc * cos + pltpu.roll(yc, LANES - AXIS_ROT // 2, 1) * sa
                          + pltpu.roll(yc, AXIS_ROT // 2, 1) * sb)
                o_ref[c, rows, :] = yc.astype(BF16)

    @pl.when(j == 0)
    def _():
        r = jnp.minimum(lax.div(row0 + i * tm, seq), n_batch)
        u_ref[...] = _prenorm(x_ref[...], g_ref[...], mod_ref, r, 0, d).astype(BF16)
        project(True)

    @pl.when(jnp.logical_and(j > 0, j < n_qk_j))
    def _():
        project(True)

    @pl.when(j >= n_qk_j)
    def _():
        project(False)


def _qkv_proj(xa, mod, g, w, rope, *, layer, ia, seq, n_batch, row0):
    rows, d = xa.shape
    n = w.shape[2]
    n_lat_rows = n_batch * seq
    n_lat_here = max(0, min(rows, n_lat_rows - row0))
    tm = _tile(1024, seq, *[c for c in (n_lat_here, rows - n_lat_here) if c])
    tn = _tile(512, n // 3)
    assert tn % LANES == 0 and row0 % tm == 0
    lat_tiles_per_seq = seq // tm
    n_lat_tiles = n_lat_here // tm
    n_q_j = n // 3 // tn

    def tab_index(i, j):
        lat = jnp.where(j < n_q_j, 0, jnp.where(j < 2 * n_q_j, 1, 2))
        ctx = jnp.where(j < n_q_j, 3, 2)
        return (jnp.where(i < n_lat_tiles, lat, ctx), 0, (row0 // tm + i) % lat_tiles_per_seq, 0)

    kern = functools.partial(_qkv_kernel, tm=tm, tn=tn, seq=seq, n_batch=n_batch, row0=row0,
                             n_qk_j=2 * n_q_j)
    return pl.pallas_call(
        kern,
        out_shape=jax.ShapeDtypeStruct((n // LANES, rows, LANES), BF16),
        grid=(rows // tm, n // tn),
        in_specs=[
            pl.BlockSpec((tm, d), lambda i, j: (i, 0)),
            _mod_spec(mod, layer, 2),
            _vec_spec(d, layer),
            pl.BlockSpec((None, d, tn), lambda i, j: (ia, 0, j)),
            pl.BlockSpec((None, 3, tm, LANES), tab_index),
        ],
        out_specs=pl.BlockSpec((tn // LANES, tm, LANES), lambda i, j: (j, i, 0)),
        scratch_shapes=[pltpu.VMEM((tm, d), BF16)],
        compiler_params=_cparams(2),
        name="qkv_proj",
    )(xa, mod, g, w, rope)


def _rope_tables(seq):
    rows = seq // GRID_W
    row = jnp.repeat(jnp.arange(rows), GRID_W).astype(F32)
    col = jnp.tile(jnp.arange(GRID_W), rows).astype(F32)
    n_freq = AXIS_ROT // 2
    inv = 1.0 / (ROPE_BASE ** (jnp.arange(n_freq, dtype=F32) / n_freq))
    ang_r = row[:, None] * inv
    ang_c = col[:, None] * inv
    cr, sr, cc, sc = jnp.cos(ang_r), jnp.sin(ang_r), jnp.cos(ang_c), jnp.sin(ang_c)
    z = jnp.zeros_like(sr)
    reps = LANES // HEAD_DIM
    cos = jnp.tile(jnp.concatenate([cr, cr, cc, cc], axis=1), (1, reps))
    sa = jnp.tile(jnp.concatenate([-sr, z, -sc, z], axis=1), (1, reps))
    sb = jnp.tile(jnp.concatenate([z, sr, z, sc], axis=1), (1, reps))
    rot = jnp.stack([cos, sa, sb])
    ident = jnp.stack([jnp.ones_like(cos), jnp.zeros_like(cos), jnp.zeros_like(cos)])
    qs = math.log2(math.e) / math.sqrt(HEAD_DIM)
    return jnp.stack([rot * qs, rot, ident, ident * qs])


def _lambda(lq1_ref, lk1_ref, lq2_ref, lk2_ref, lam_init):
    return (jnp.exp(jnp.sum(lq1_ref[...] * lk1_ref[...], axis=-1, keepdims=True))
            - jnp.exp(jnp.sum(lq2_ref[...] * lk2_ref[...], axis=-1, keepdims=True)) + lam_init)


def _attn_kernel(*refs, n_seg, lam_init, rb, n_cast, n_mod):
    refs = list(refs)
    take = lambda n: [refs.pop(0) for _ in range(n)]
    (q_ref,) = take(1)
    seg_refs = take(2 * n_seg)
    lq1_ref, lk1_ref, lq2_ref, lk2_ref, gs_ref = take(5)
    cast_in = take(n_cast)
    cond_ref, = take(1) if n_mod else (None,)
    wm_refs, bm_refs = take(n_mod), take(n_mod)
    (o_ref,) = take(1)
    cast_out, mod_out = take(n_cast), take(n_mod)
    kk2_ref, va2_ref = refs

    for w_in, w_out in zip(cast_in, cast_out):
        w_out[...] = w_in[...].astype(BF16)
    if n_mod:
        sc = jax.nn.silu(cond_ref[...]).astype(BF16)
        for wm_ref, bm_ref, mo_ref in zip(wm_refs, bm_refs, mod_out):
            mo_ref[...] = (jnp.dot(sc, wm_ref[...].astype(BF16), preferred_element_type=F32)
                           + bm_ref[...])

    hb, q_len, _ = q_ref.shape
    n_blocks = q_len // rb
    lam = _lambda(lq1_ref, lk1_ref, lq2_ref, lk2_ref, lam_init)
    gs = gs_ref[...]
    lane = lax.broadcasted_iota(jnp.int32, (rb, LANES), 1)
    va2_ref[:, :, V_HEAD_DIM:] = jnp.ones(va2_ref.shape[:2] + (V_HEAD_DIM,), BF16)

    for h in range(hb):
        kk_ref, va_ref = kk2_ref.at[h % 2], va2_ref.at[h % 2]
        off = 0
        for s in range(n_seg):
            n = seg_refs[2 * s].shape[1]
            kk_ref[off:off + n, :] = seg_refs[2 * s][h]
            va_ref[off:off + n, 0:V_HEAD_DIM] = seg_refs[2 * s + 1][h]
            off += n

        def scores(r):
            q = q_ref[h, r * rb:(r + 1) * rb, :]
            qq = jnp.concatenate([jnp.where(lane < HEAD_DIM, q, jnp.zeros_like(q)),
                                  jnp.where(lane >= HEAD_DIM, q, jnp.zeros_like(q))], axis=0)
            return lax.dot_general(qq, kk_ref[...], (((1,), (1,)), ((), ())),
                                   preferred_element_type=F32)

        def finish(r, s):
            p = jnp.exp2(s - jnp.max(s, axis=-1, keepdims=True)).astype(BF16)
            acc = jnp.dot(p, va_ref[...], preferred_element_type=F32)
            ratio = acc[:, :V_HEAD_DIM] / acc[:, V_HEAD_DIM:]
            o = ratio[:rb] - lam * ratio[rb:]
            o = _rms(o, gs, SUBLN_EPS) * (1.0 - lam_init)
            o_ref[r * rb:(r + 1) * rb, h * V_HEAD_DIM:(h + 1) * V_HEAD_DIM] = o.astype(BF16)

        s_next = scores(0)
        for r in range(n_blocks):
            s_cur = s_next
            if r + 1 < n_blocks:
                s_next = scores(r + 1)
            finish(r, s_cur)


def _attention(q_arr, q_row0, q_len, segs, lam_params, gs, *, ia, lam_init, n_batch, hb,
               cast=None, mods=None):
    n_heads = q_arr.shape[0] // 3
    assert n_heads % hb == 0 and q_row0 % q_len == 0
    nk = sum(n for _, _, n in segs)
    q_blk0 = q_row0 // q_len
    in_specs = [pl.BlockSpec((hb, q_len, LANES), lambda b, h: (h, q_blk0 + b, 0))]
    args = [q_arr]
    for arr, row0, n in segs:
        assert row0 % n == 0 and arr.shape[0] == 3 * n_heads
        for part in (1, 2):
            in_specs.append(pl.BlockSpec(
                (hb, n, LANES),
                lambda b, h, blk0=row0 // n, col0=part * n_heads // hb: (col0 + h, blk0 + b, 0)))
            args.append(arr)
    in_specs += [_vec_spec(HEAD_DIM, ia)] * 4
    in_specs.append(_vec_spec(V_HEAD_DIM, ia))
    args += [*lam_params, gs]
    out_shape = [jax.ShapeDtypeStruct((n_batch * q_len, n_heads * V_HEAD_DIM), BF16)]
    out_specs = [pl.BlockSpec((q_len, hb * V_HEAD_DIM), lambda b, h: (b, h))]
    n_hg = n_heads // hb
    n_steps = n_batch * n_hg
    step = lambda b, h: b * n_hg + h
    n_cast = n_mod = 0
    if cast is not None:
        layer, stacks = cast
        n_cast = len(stacks)
        for w in stacks:
            _, rows, cols = w.shape
            assert rows % n_steps == 0
            rblk = rows // n_steps
            in_specs.append(pl.BlockSpec((None, rblk, cols),
                                         lambda b, h, layer=layer: (layer, step(b, h), 0)))
            args.append(w)
            out_shape.append(jax.ShapeDtypeStruct((rows, cols), BF16))
            out_specs.append(pl.BlockSpec((rblk, cols), lambda b, h: (step(b, h), 0)))
    if mods is not None:
        cond, w_mod, b_mod3, layers = mods
        n_mod = len(layers)
        _, dm, nm = w_mod.shape
        tnm = min(t for t in range(LANES, nm + 1, LANES) if nm % t == 0 and nm // t <= n_steps)
        blk = lambda b, h: jnp.minimum(step(b, h), nm // tnm - 1)
        in_specs.append(pl.BlockSpec(cond.shape, lambda b, h: (0, 0)))
        args.append(cond)
        for l in layers:
            in_specs.append(pl.BlockSpec((None, dm, tnm), lambda b, h, l=l: (l, 0, blk(b, h))))
            args.append(w_mod)
        for l in layers:
            in_specs.append(pl.BlockSpec((None, 1, tnm), lambda b, h, l=l: (l, 0, blk(b, h))))
            args.append(b_mod3)
            out_shape.append(jax.ShapeDtypeStruct((1, cond.shape[0], nm), F32))
            out_specs.append(pl.BlockSpec((None, cond.shape[0], tnm),
                                          lambda b, h: (0, 0, blk(b, h))))
    outs = pl.pallas_call(
        functools.partial(_attn_kernel, n_seg=len(segs), lam_init=lam_init,
                          rb=_tile(128, q_len), n_cast=n_cast, n_mod=n_mod),
        out_shape=out_shape,
        grid=(n_batch, n_hg),
        in_specs=in_specs,
        out_specs=out_specs,
        scratch_shapes=[pltpu.VMEM((2, nk, LANES), BF16),
                        pltpu.VMEM((2, nk, 2 * V_HEAD_DIM), BF16)],
        compiler_params=_cparams(2),
        name="diff_attn",
    )(*args)
    return outs[0], outs[1:1 + n_cast], outs[1 + n_cast:]


def _proj_res_kernel(*refs, tm, seq, n_batch, n_lat_tiles, has_ctx, split_x):
    refs = list(refs)
    inl_ref = refs.pop(0)
    inc_ref = refs.pop(0) if has_ctx else None
    xl_ref = refs.pop(0)
    xc_ref = refs.pop(0) if split_x else xl_ref
    mod_ref, g_ref, w_ref, o_ref = refs
    i = pl.program_id(0)
    d = o_ref.shape[1]

    def body(inp_ref, x_ref):
        r = jnp.minimum(lax.div(i * tm, seq), n_batch)
        gg = g_ref[...] * _mod_row(mod_ref, r, 2, d)
        rb = _tile(256, tm)
        for b in range(tm // rb):
            rows = slice(b * rb, (b + 1) * rb)
            y = jnp.dot(inp_ref[rows, :], w_ref[...], preferred_element_type=F32)
            inv = lax.rsqrt(jnp.mean(y * y, axis=-1, keepdims=True) + NORM_EPS)
            o_ref[rows, :] = x_ref[rows, :] + (y * inv) * gg

    if has_ctx:
        @pl.when(i < n_lat_tiles)
        def _():
            body(inl_ref, xl_ref)

        @pl.when(i >= n_lat_tiles)
        def _():
            body(inc_ref, xc_ref)
    else:
        body(inl_ref, xl_ref)


def _proj_res(inp_l, inp_c, x_l, x_c, mod, g, w, *, layer, iw, seq, n_batch):
    d = x_l.shape[1]
    n_lat_rows = inp_l.shape[0]
    has_ctx = inp_c is not None
    split_x = x_c is not None
    assert has_ctx or not split_x
    rows = n_lat_rows + (inp_c.shape[0] if has_ctx else 0)
    tm = _tile(512, seq, *([inp_c.shape[0]] if has_ctx else []))
    n_lat_tiles = n_lat_rows // tm
    kin = inp_l.shape[1]
    lat_index = lambda i: (jnp.minimum(i, n_lat_tiles - 1), 0)
    ctx_index = lambda i: (jnp.maximum(i - n_lat_tiles, 0), 0)
    in_specs = [pl.BlockSpec((tm, kin), lat_index)]
    args = [inp_l]
    if has_ctx:
        in_specs.append(pl.BlockSpec((tm, kin), ctx_index))
        args.append(inp_c)
    if split_x:
        in_specs += [pl.BlockSpec((tm, d), lat_index), pl.BlockSpec((tm, d), ctx_index)]
        args += [x_l, x_c]
    else:
        in_specs.append(pl.BlockSpec((tm, d), lambda i: (i, 0)))
        args.append(x_l)
    in_specs += [
        _mod_spec(mod, layer, 1),
        _vec_spec(d, layer),
        pl.BlockSpec((None,) + w.shape[1:], lambda i: (iw, 0, 0)),
    ]
    args += [mod, g, w]
    kern = functools.partial(_proj_res_kernel, tm=tm, seq=seq, n_batch=n_batch,
                             n_lat_tiles=n_lat_tiles, has_ctx=has_ctx, split_x=split_x)
    return pl.pallas_call(
        kern,
        out_shape=jax.ShapeDtypeStruct((rows, d), F32),
        grid=(rows // tm,),
        in_specs=in_specs,
        out_specs=pl.BlockSpec((tm, d), lambda i: (i, 0)),
        compiler_params=_cparams(1),
        name="proj_res",
    )(*args)


def _mlp_kernel(*refs, tm, rb, seq, n_batch, cast_next, single_chunk):
    if cast_next:
        (x_ref, mod_ref, gpre_ref, gpost_ref, w1_ref, w2_ref, w1n_ref, w2n_ref,
         o_ref, w1o_ref, w2o_ref, u_ref) = refs
    else:
        x_ref, mod_ref, gpre_ref, gpost_ref, w1_ref, w2_ref, o_ref, u_ref = refs
    i = pl.program_id(0)
    j = pl.program_id(1)
    d = x_ref.shape[1]
    r = jnp.minimum(lax.div(i * tm, seq), n_batch)

    @pl.when(j == 0)
    def _():
        u_ref[...] = _prenorm(x_ref[...], gpre_ref[...], mod_ref, r, 3, d).astype(BF16)

    def hidden(b):
        h = jnp.dot(u_ref[b * rb:(b + 1) * rb, :], w1_ref[...], preferred_element_type=F32)
        h = jnp.maximum(h, 0.0)
        return (h * h).astype(BF16)

    def chunk(first, last):
        n_blocks = tm // rb
        if last:
            gg = gpost_ref[...] * _mod_row(mod_ref, r, 5, d)
        h_next = hidden(0)
        for b in range(n_blocks):
            h_cur = h_next
            if b + 1 < n_blocks:
                h_next = hidden(b + 1)
            rows = slice(b * rb, (b + 1) * rb)
            y = jnp.dot(h_cur, w2_ref[...], preferred_element_type=F32)
            if not first:
                y = o_ref[rows, :] + y
            if last:
                inv = lax.rsqrt(jnp.mean(y * y, axis=-1, keepdims=True) + NORM_EPS)
                y = x_ref[rows, :] + (y * inv) * gg
            o_ref[rows, :] = y
        if cast_next:
            w1o_ref[...] = w1n_ref[...].astype(BF16)
            w2o_ref[...] = w2n_ref[...].astype(BF16)

    n_j = pl.num_programs(1)

    @pl.when(j == 0)
    def _():
        chunk(True, single_chunk)

    if not single_chunk:
        @pl.when(jnp.logical_and(j > 0, j < n_j - 1))
        def _():
            chunk(False, False)

        @pl.when(j == n_j - 1)
        def _():
            chunk(False, True)


CAST_BLOCK_BYTES = 1024 * 1024


def _mlp(xu, mod, gpre, gpost, w1, w2, w_next, *, layer, seq, n_batch, rows):
    d = xu.shape[1]
    dff = w1.shape[1]
    tm = _tile(1024, seq, *([xu.shape[0] - n_batch * seq] if rows > n_batch * seq else []))
    tf = _tile(512, dff)
    n_j = dff // tf
    n_steps = (rows // tm) * n_j
    cast_next = w_next is not None
    kern = functools.partial(_mlp_kernel, tm=tm, rb=_tile(512, tm), seq=seq, n_batch=n_batch,
                             cast_next=cast_next, single_chunk=n_j == 1)
    in_specs = [
        pl.BlockSpec((tm, d), lambda i, j: (i, 0)),
        _mod_spec(mod, layer, 2),
        _vec_spec(d, layer),
        _vec_spec(d, layer),
        pl.BlockSpec((d, tf), lambda i, j: (0, j)),
        pl.BlockSpec((tf, d), lambda i, j: (j, 0)),
    ]
    args = [xu, mod, gpre, gpost, w1, w2]
    out_shape = [jax.ShapeDtypeStruct((rows, d), F32)]
    out_specs = [pl.BlockSpec((tm, d), lambda i, j: (i, 0))]
    if cast_next:
        r1 = _tile(CAST_BLOCK_BYTES // (4 * dff), d)
        r2 = _tile(CAST_BLOCK_BYTES // (4 * d), dff)
        c1, c2 = d // r1, dff // r2
        assert c1 + c2 <= n_steps

        def blk1(i, j):
            return jnp.minimum(i * n_j + j, c1 - 1)

        def blk2(i, j):
            return jnp.clip(i * n_j + j - c1, 0, c2 - 1)

        in_specs += [pl.BlockSpec((None, r1, dff), lambda i, j: (layer + 1, blk1(i, j), 0)),
                     pl.BlockSpec((None, r2, d), lambda i, j: (layer + 1, blk2(i, j), 0))]
        args += list(w_next)
        out_shape += [jax.ShapeDtypeStruct((d, dff), BF16), jax.ShapeDtypeStruct((dff, d), BF16)]
        out_specs += [pl.BlockSpec((r1, dff), lambda i, j: (blk1(i, j), 0)),
                      pl.BlockSpec((r2, d), lambda i, j: (blk2(i, j), 0))]
    outs = pl.pallas_call(
        kern,
        out_shape=out_shape,
        grid=(rows // tm, n_j),
        in_specs=in_specs,
        out_specs=out_specs,
        scratch_shapes=[pltpu.VMEM((tm, d), BF16)],
        compiler_params=_cparams(2),
        name="sq_relu_mlp",
    )(*args)
    return outs if cast_next else (outs[0], None, None)


def _dft_tables(n):
    b = max(f for f in range(1, math.isqrt(n) + 1) if n % f == 0)
    a = n // b
    j = jnp.arange(n, dtype=jnp.int32)[:, None]
    ang1 = ((j * jnp.arange(a, dtype=jnp.int32)[None, :]) % a).astype(F32) * (2.0 * math.pi / a)
    ang0 = ((j * jnp.arange(b, dtype=jnp.int32)[None, :]) % n).astype(F32) * (2.0 * math.pi / n)
    c1, s1 = jnp.cos(ang1)[:, :, None], jnp.sin(ang1)[:, :, None]
    c0, s0 = jnp.cos(ang0)[:, None, :], jnp.sin(ang0)[:, None, :]
    scale = 1.0 / math.sqrt(n)
    cos = (c1 * c0 - s1 * s0).reshape(n, n) * scale
    sin = (s1 * c0 + c1 * s0).reshape(n, n) * scale
    return cos, sin


def _fourier_ch_kernel(x_ref, mod_ref, g_ref, cs_ref, a_ref, b_ref, *, tm, seq, n_batch):
    i = pl.program_id(0)
    d = x_ref.shape[1]
    cg = d // N_FFT_GROUPS
    r = jnp.minimum(lax.div(i * tm, seq), n_batch)
    u = _prenorm(x_ref[...], g_ref[...], mod_ref, r, 0, d).astype(BF16)
    for g in range(N_FFT_GROUPS):
        ab = jnp.dot(u[:, g * cg:(g + 1) * cg], cs_ref[...], preferred_element_type=F32)
        a_ref[:, g * cg:(g + 1) * cg] = ab[:, :cg].astype(BF16)
        b_ref[:, g * cg:(g + 1) * cg] = ab[:, cg:].astype(BF16)


def _fourier_channels(xu, mod, g, cs, *, layer, seq, n_batch, rows):
    d = xu.shape[1]
    tm = _tile(512, seq, *([xu.shape[0] - n_batch * seq] if rows > n_batch * seq else []))
    kern = functools.partial(_fourier_ch_kernel, tm=tm, seq=seq, n_batch=n_batch)
    return pl.pallas_call(
        kern,
        out_shape=[jax.ShapeDtypeStruct((rows, d), BF16)] * 2,
        grid=(rows // tm,),
        in_specs=[
            pl.BlockSpec((tm, d), lambda i: (i, 0)),
            _mod_spec(mod, layer, 1),
            _vec_spec(d, layer),
            pl.BlockSpec(cs.shape, lambda i: (0, 0)),
        ],
        out_specs=[pl.BlockSpec((tm, d), lambda i: (i, 0))] * 2,
        compiler_params=_cparams(1),
        name="fourier_channels",
    )(xu, mod, g, cs)


def _fourier_seq_kernel(tc_ref, tms_ref, a_ref, b_ref, o_ref, *, tmm):
    n = tc_ref.shape[0]
    for mi in range(n // tmm):
        rows = slice(mi * tmm, (mi + 1) * tmm)
        y = (jnp.dot(tc_ref[rows, :], a_ref[...], preferred_element_type=F32)
             + jnp.dot(tms_ref[rows, :], b_ref[...], preferred_element_type=F32))
        o_ref[rows, :] = y.astype(BF16)


def _fourier_seq(a, b, tc, tms, *, n_batch, length, row0):
    d = a.shape[1]
    tn = _tile(512, d)
    tmm = _tile(512, length)
    blk0 = row0 // length
    assert row0 % length == 0
    ab_spec = pl.BlockSpec((length, tn), lambda bi, ni: (blk0 + bi, ni))
    tab_spec = pl.BlockSpec((length, length), lambda bi, ni: (0, 0))
    return pl.pallas_call(
        functools.partial(_fourier_seq_kernel, tmm=tmm),
        out_shape=jax.ShapeDtypeStruct((n_batch * length, d), BF16),
        grid=(n_batch, d // tn),
        in_specs=[tab_spec, tab_spec, ab_spec, ab_spec],
        out_specs=pl.BlockSpec((length, tn), lambda bi, ni: (bi, ni)),
        compiler_params=_cparams(2),
        name="fourier_seq",
    )(tc, tms, a, b)


def _pool_kernel(x_ref, xp_ref, xn_ref, mod_ref, gpre_ref, gpost_ref, wp_ref, ps_ref, o_ref,
                 ext_ref, y_ref, *, tm, seq, ctx_len, n_batch, n_lat_rows):
    i = pl.program_id(0)
    d = x_ref.shape[1]
    cg = d // len(POOL_WINDOWS)
    row0 = i * tm
    is_lat = row0 < n_lat_rows
    slen = jnp.where(is_lat, seq, ctx_len)
    pos0 = jnp.where(is_lat, lax.rem(row0, seq), lax.rem(row0 - n_lat_rows, ctx_len))
    r = jnp.minimum(lax.div(row0, seq), n_batch)
    g_pre = gpre_ref[...]
    keep_prev = jnp.where(pos0 > 0, 1.0, 0.0).astype(F32)
    keep_next = jnp.where(pos0 + tm < slen, 1.0, 0.0).astype(F32)
    ext_ref[0:POOL_HALO, :] = _prenorm(xp_ref[...], g_pre, mod_ref, r, 0, d) * keep_prev
    ext_ref[POOL_HALO:POOL_HALO + tm, :] = _prenorm(x_ref[...], g_pre, mod_ref, r, 0, d)
    ext_ref[POOL_HALO + tm:, :] = _prenorm(xn_ref[...], g_pre, mod_ref, r, 0, d) * keep_next

    pos = pos0 + lax.broadcasted_iota(jnp.int32, (tm, 1), 0)
    for g, w in enumerate(POOL_WINDOWS):
        cols = slice(g * cg, (g + 1) * cg)
        acc = ext_ref[POOL_HALO - w // 2:POOL_HALO - w // 2 + tm, cols]
        for dlt in range(-w // 2 + 1, w // 2):
            acc = acc + ext_ref[POOL_HALO + dlt:POOL_HALO + dlt + tm, cols]
        lo = jnp.maximum(pos - w // 2, 0)
        hi = jnp.minimum(pos - w // 2 + w, slen)
        mean = acc / (hi - lo).astype(F32)
        diff = (mean - ext_ref[POOL_HALO:POOL_HALO + tm, cols]).astype(BF16)
        y_ref[:, cols] = jnp.dot(diff, wp_ref[g], preferred_element_type=F32) * ps_ref[:, cols]

    gate = _mod_row(mod_ref, r, 2, d)
    o_ref[...] = x_ref[...] + gate * _rms(y_ref[...], gpost_ref[...], NORM_EPS)


def _pool_mix(xu, mod, gpre, gpost, wp, ps, *, layer, ic, seq, ctx_len, n_batch, rows):
    d = xu.shape[1]
    n_lat_rows = n_batch * seq
    tm = _tile(256, seq, *([ctx_len] if rows > n_lat_rows else []))
    assert tm % POOL_HALO == 0 and POOL_HALO >= max(POOL_WINDOWS) // 2
    hb = tm // POOL_HALO
    last_hblk = xu.shape[0] // POOL_HALO - 1
    kern = functools.partial(_pool_kernel, tm=tm, seq=seq, ctx_len=ctx_len, n_batch=n_batch,
                             n_lat_rows=n_lat_rows)
    return pl.pallas_call(
        kern,
        out_shape=jax.ShapeDtypeStruct((rows, d), F32),
        grid=(rows // tm,),
        in_specs=[
            pl.BlockSpec((tm, d), lambda i: (i, 0)),
            pl.BlockSpec((POOL_HALO, d), lambda i: (jnp.maximum(i * hb - 1, 0), 0)),
            pl.BlockSpec((POOL_HALO, d), lambda i: (jnp.minimum((i + 1) * hb, last_hblk), 0)),
            _mod_spec(mod, layer, 1),
            _vec_spec(d, layer),
            _vec_spec(d, layer),
            pl.BlockSpec((None,) + wp.shape[1:], lambda i: (ic, 0, 0, 0)),
            _vec_spec(d, ic),
        ],
        out_specs=pl.BlockSpec((tm, d), lambda i: (i, 0)),
        scratch_shapes=[pltpu.VMEM((tm + 2 * POOL_HALO, d), F32), pltpu.VMEM((tm, d), F32)],
        compiler_params=_cparams(1),
        name="pool_mix",
    )(xu, xu, xu, mod, gpre, gpost, wp, ps)


def kernel(x, c, ctx, c_ctx, w_mod, b_mod, g_mix_pre, g_mix_post, g_mlp_pre, g_mlp_post,
           w_qkv, w_attn_out, lambda_q1, lambda_k1, lambda_q2, lambda_k2, g_subln,
           w_fourier_out, w_pool, pool_scale, w_mlp_in, w_mlp_out):
    n_batch, seq, d = x.shape
    ctx_len = ctx.shape[1]
    depth = w_mod.shape[0]
    n_lat_rows = n_batch * seq
    n_ctx_rows = n_batch * ctx_len
    n_rows = n_lat_rows + n_ctx_rows
    n_heads = w_attn_out.shape[1] // V_HEAD_DIM

    n_cond = -(-(n_batch + 1) // 8) * 8
    cond = jnp.concatenate(
        [c, c_ctx[None, :], jnp.zeros((n_cond - n_batch - 1, d), F32)], axis=0)
    b_mod3 = b_mod.reshape(depth, 1, b_mod.shape[1])
    mods = [_mod_layers(cond, w_mod, b_mod3, 1)] + [None] * (depth - 1)

    w_o_b = w_attn_out.astype(BF16)
    w_f_b = w_fourier_out.astype(BF16)
    w_p_b = w_pool.astype(BF16)
    w1_b = w2_b = None

    row3 = lambda t: t.reshape(t.shape[0], 1, t.shape[1])
    g_mix_pre, g_mix_post, g_mlp_pre, g_mlp_post = map(
        row3, (g_mix_pre, g_mix_post, g_mlp_pre, g_mlp_post))
    lambda_q1, lambda_k1, lambda_q2, lambda_k2, g_subln, pool_scale = map(
        row3, (lambda_q1, lambda_k1, lambda_q2, lambda_k2, g_subln, pool_scale))
    lam_params = (lambda_q1, lambda_k1, lambda_q2, lambda_k2)

    rope = _rope_tables(seq)
    common = dict(seq=seq, n_batch=n_batch)

    x_l = x.reshape(n_lat_rows, d)
    x_c = ctx.reshape(n_ctx_rows, d)
    xu = None

    ia = ib = ic = 0
    for i in range(depth):
        last = i == depth - 1
        kind = i % N_MIXERS
        out_rows = n_lat_rows if last else n_rows
        mod = mods[i]
        if xu is None and kind != 0:
            xu = jnp.concatenate([x_l, x_c], axis=0)
        if kind == 0:
            lam_init = 0.8 - 0.6 * math.exp(-0.3 * i)
            qkw = dict(layer=i, ia=ia, **common)
            akw = dict(ia=ia, lam_init=lam_init, n_batch=n_batch)
            if xu is None:
                qkv_l = _qkv_proj(x_l, mod, g_mix_pre, w_qkv, rope, row0=0, **qkw)
                qkv_c = _qkv_proj(x_c, mod, g_mix_pre, w_qkv, rope, row0=n_lat_rows, **qkw)
                ctx_row0 = 0
            else:
                qkv_l = qkv_c = _qkv_proj(xu, mod, g_mix_pre, w_qkv, rope, row0=0, **qkw)
                ctx_row0 = n_lat_rows
            ctx_seg = (qkv_c, ctx_row0, ctx_len)
            cast = (i, [w_mlp_in, w_mlp_out]) if w1_b is None else None
            todo = [l for l in range(depth) if mods[l] is None]
            o_l, cast_out, mod_out = _attention(
                qkv_l, 0, seq, [(qkv_l, 0, seq), ctx_seg], lam_params, g_subln,
                hb=2 if n_heads % 2 == 0 else 1, cast=cast,
                mods=(cond, w_mod, b_mod3, todo) if todo else None, **akw)
            if cast is not None:
                w1_b, w2_b = cast_out
            for l, m in zip(todo, mod_out):
                mods[l] = m
            o_c = None if last else _attention(
                qkv_c, ctx_row0, ctx_len, [ctx_seg], lam_params, g_subln, hb=n_heads, **akw)[0]
            if xu is None:
                xu = _proj_res(o_l, o_c, x_l, None if last else x_c, mod, g_mix_post, w_o_b,
                               layer=i, iw=ia, **common)
            else:
                xu = _proj_res(o_l, o_c, xu, None, mod, g_mix_post, w_o_b,
                               layer=i, iw=ia, **common)
            ia += 1
        elif kind == 1:
            cg = d // N_FFT_GROUPS
            cc, sc = _dft_tables(cg)
            cs = jnp.concatenate([cc, sc], axis=1).astype(BF16)
            a, b = _fourier_channels(xu, mod, g_mix_pre, cs, layer=i, rows=out_rows, **common)
            tc, ts = _dft_tables(seq)
            y_l = _fourier_seq(a, b, tc.astype(BF16), (-ts).astype(BF16),
                               n_batch=n_batch, length=seq, row0=0)
            y_c = None
            if not last:
                tcc, tsc = _dft_tables(ctx_len)
                y_c = _fourier_seq(a, b, tcc.astype(BF16), (-tsc).astype(BF16),
                                   n_batch=n_batch, length=ctx_len, row0=n_lat_rows)
            xu = _proj_res(y_l, y_c, xu, None, mod, g_mix_post, w_f_b, layer=i, iw=ib, **common)
            ib += 1
        else:
            xu = _pool_mix(xu, mod, g_mix_pre, g_mix_post, w_p_b, pool_scale, layer=i, ic=ic,
                           ctx_len=ctx_len, rows=out_rows, **common)
            ic += 1
        w_next = None if last else (w_mlp_in, w_mlp_out)
        xu, w1_b, w2_b = _mlp(xu, mod, g_mlp_pre, g_mlp_post, w1_b, w2_b, w_next, layer=i,
                              rows=out_rows, **common)
    return xu[:n_lat_rows].reshape(n_batch, seq, d)
```
